```python
import jax, jax.numpy as jnp
from jax import lax
import numpy as np

D_MODEL = 1024
BATCH = 16
SEQ = 2048
DEPTH = 4
DEC_BATCH = 2
DEC_SEQ = 8192
PAST_LEN = 128

N_MIXERS = 2
N_A_LAYERS = (DEPTH + 1) // 2
N_B_LAYERS = DEPTH // 2
N_MEM = 256
X_HEADS = 4
X_HD = D_MODEL // 8
X_W = X_HEADS * X_HD
A_HEADS = 4
A_QK = D_MODEL // 8
A_V = D_MODEL // 4
A_CHUNK = 64
A_SIZES = [A_HEADS * A_QK, A_HEADS * A_QK, A_HEADS * A_V, A_HEADS * A_V, 4 * A_HEADS]
A_REC = sum(A_SIZES)
A_PROJ = A_REC + X_W
A_OUT = A_HEADS * A_V + X_W
B_EXPAND = 128
B_HEADS = D_MODEL // B_EXPAND
B_F = B_EXPAND
B_V = D_MODEL // B_HEADS
B_CHUNK = 32
B_SIZES = [B_HEADS * B_F, B_HEADS * B_V, B_HEADS * B_F, B_HEADS * B_F, B_HEADS * B_V]
B_REC = sum(B_SIZES)
B_PROJ = B_REC + X_W
B_OUT = B_HEADS * B_V + X_W
D_FF = ((8 * D_MODEL // 3 + 127) // 128) * 128
CONV_W = 3
EPS = 1e-6

kernel_name = "hybrid_mlstm_hgrn2_memxattn_encoder"


def _offsets(sizes):
    return [int(v) for v in np.cumsum(sizes)[:-1]]


def _rmsnorm(x, g):
    xf = x.astype(jnp.float32)
    y = xf * lax.rsqrt(jnp.mean(xf * xf, axis=-1, keepdims=True) + EPS) * g.astype(jnp.float32)
    return y.astype(x.dtype)


def _head_rmsnorm(h, g):
    H_, d = h.shape[-2:]
    y = h * lax.rsqrt(jnp.mean(h * h, axis=-1, keepdims=True) + EPS) * g.astype(jnp.float32).reshape(H_, d)
    return y.reshape(h.shape[:2] + (H_ * d,))


def _to_chunks(t, L):
    B_, S_, H_ = t.shape[:3]
    t = t.reshape((B_, S_ // L, L, H_) + t.shape[3:])
    return jnp.moveaxis(jnp.moveaxis(t, 1, 0), 3, 2)


def _from_chunks(t):
    nc, B_, H_, L_, d = t.shape
    t = jnp.moveaxis(jnp.moveaxis(t, 0, 1), 2, 3)
    return t.reshape(B_, nc * L_, H_, d)


def _flip(t):
    return jnp.flip(t, axis=1)


def _mlstm_scan(q, k, v, li, lf):
    B_, S_, H_, dk = q.shape
    dv = v.shape[-1]
    L = A_CHUNK
    mask = jnp.tril(jnp.ones((L, L), dtype=bool))
    xs = (_to_chunks(q, L), _to_chunks(k, L), _to_chunks(v, L), _to_chunks(li, L), _to_chunks(lf, L))

    def step(carry, inp):
        C, n, m = carry
        qc, kc, vc, lic, lfc = inp
        b = jnp.cumsum(lfc, axis=-1)
        Dm = jnp.where(mask, b[..., :, None] - b[..., None, :] + lic[..., None, :], -jnp.inf)
        inter = b + m[..., None]
        mt = jnp.maximum(inter, Dm.max(axis=-1))
        s = jnp.einsum('bhtd,bhsd->bhts', qc, kc) * jnp.exp(Dm - mt[..., None])
        ei = jnp.exp(inter - mt)
        num = jnp.einsum('bhts,bhsv->bhtv', s, vc) + ei[..., None] * jnp.einsum('bhtd,bhdv->bhtv', qc, C)
        den = s.sum(axis=-1) + ei * jnp.einsum('bhtd,bhd->bht', qc, n)
        h = num / jnp.maximum(jnp.abs(den), jnp.exp(-mt))[..., None]
        bl = b[..., -1]
        g = bl[..., None] - b + lic
        m_new = jnp.maximum(bl + m, g.max(axis=-1))
        wk = jnp.exp(g - m_new[..., None])
        dec = jnp.exp(bl + m - m_new)
        C_new = dec[..., None, None] * C + jnp.einsum('bhs,bhsd,bhsv->bhdv', wk, kc, vc)
        n_new = dec[..., None] * n + jnp.einsum('bhs,bhsd->bhd', wk, kc)
        return (C_new, n_new, m_new), h

    init = (jnp.zeros((B_, H_, dk, dv), jnp.float32), jnp.zeros((B_, H_, dk), jnp.float32),
            jnp.zeros((B_, H_), jnp.float32))
    _, hs = lax.scan(step, init, xs)
    return _from_chunks(hs)


def _mlstm_mixer(p, b_gate, g_head):
    B_, S_, _ = p.shape
    f32 = jnp.float32
    q, k, v, o, gates = jnp.split(p, _offsets(A_SIZES), axis=-1)
    q = q.reshape(B_, S_, A_HEADS, A_QK).astype(f32) * (A_QK ** -0.5)
    k = k.reshape(B_, S_, A_HEADS, A_QK).astype(f32)
    v = v.reshape(B_, S_, A_HEADS, A_V).astype(f32)
    gates = gates.astype(f32) + b_gate.astype(f32)
    ig_f, fg_f, ig_b, fg_b = jnp.split(gates, 4, axis=-1)
    h_f = _mlstm_scan(q, k, v, ig_f, jax.nn.log_sigmoid(fg_f))
    h_b = _flip(_mlstm_scan(_flip(q), _flip(k), _flip(v), _flip(ig_b), _flip(jax.nn.log_sigmoid(fg_b))))
    h = _head_rmsnorm(h_f + h_b, g_head) * jax.nn.sigmoid(o.astype(f32))
    return h.astype(p.dtype)


def _hgrn2_scan(q, k, v, lg):
    B_, S_, H_, F_ = q.shape
    V_ = v.shape[-1]
    L = B_CHUNK
    mask = jnp.tril(jnp.ones((L, L), dtype=bool))[:, :, None]
    xs = (_to_chunks(q, L), _to_chunks(k, L), _to_chunks(v, L), _to_chunks(lg, L))

    def step(S, inp):
        qc, kc, vc, lgc = inp
        b = jnp.cumsum(lgc, axis=2)
        o = jnp.einsum('bhtf,bhfv->bhtv', qc * jnp.exp(b), S)
        diff = b[:, :, :, None, :] - b[:, :, None, :, :]
        dec = jnp.exp(jnp.where(mask, diff, -jnp.inf))
        a = jnp.einsum('bhtf,bhtsf,bhsf->bhts', qc, dec, kc)
        o = o + jnp.einsum('bhts,bhsv->bhtv', a, vc)
        bl = b[:, :, -1]
        S_new = jnp.exp(bl)[..., None] * S + jnp.einsum('bhsf,bhsv->bhfv', kc * jnp.exp(bl[:, :, None] - b), vc)
        return S_new, o

    _, os_ = lax.scan(step, jnp.zeros((B_, H_, F_, V_), jnp.float32), xs)
    return _from_chunks(os_)


def _hgrn2_mixer(p, lb_f, lb_b, g_head):
    B_, S_, _ = p.shape
    f32 = jnp.float32
    q, i, zf, zb, g = jnp.split(p, _offsets(B_SIZES), axis=-1)
    q = q.reshape(B_, S_, B_HEADS, B_F).astype(f32)
    v = i.reshape(B_, S_, B_HEADS, B_V).astype(f32)

    def gate(z, lb):
        z = z.reshape(B_, S_, B_HEADS, B_F).astype(f32)
        lb = lb.reshape(B_HEADS, B_F)
        lg = jnp.log(lb + (1.0 - lb) * jax.nn.sigmoid(z))
        kk = (1.0 - lb) * jax.nn.sigmoid(-z)
        return lg, kk

    lg_f, k_f = gate(zf, lb_f)
    lg_b, k_b = gate(zb, lb_b)
    o_f = _hgrn2_scan(q, k_f, v, lg_f)
    o_b = _flip(_hgrn2_scan(_flip(q), _flip(k_b), _flip(v), _flip(lg_b)))
    h = _head_rmsnorm(o_f + o_b, g_head) * jax.nn.silu(g.astype(f32))
    return h.astype(p.dtype)


def _cross_attention(xq, mem_n, w_kv):
    B_, S_, _ = xq.shape
    q = xq.reshape(B_, S_, X_HEADS, X_HD)
    k, v = jnp.split(mem_n @ w_kv, 2, axis=-1)
    k = k.reshape(B_, -1, X_HEADS, X_HD)
    v = v.reshape(B_, -1, X_HEADS, X_HD)
    s = jnp.einsum('bshd,bmhd->bhsm', q, k).astype(jnp.float32) * (X_HD ** -0.5)
    pr = jax.nn.softmax(s, axis=-1).astype(v.dtype)
    return jnp.einsum('bhsm,bmhd->bshd', pr, v).reshape(B_, S_, X_W)


def _conv_glu_ffn(h, w_up, conv_w, conv_b, w_down):
    u = h @ w_up
    up = jnp.pad(u, ((0, 0), (1, 1), (0, 0)))
    c = up[:, :-2] * conv_w[0] + up[:, 1:-1] * conv_w[1] + up[:, 2:] * conv_w[2] + conv_b
    a, g = jnp.split(c, 2, axis=-1)
    return (jax.nn.silu(g) * a) @ w_down


def _encoder_stack(x, mem, g_pre_mix, g_post_mix, g_mem, a_w_in, a_b_gate, a_g_head, a_w_out,
                   b_w_in, b_lb, b_g_head, b_w_out, x_w_kv, g_pre_ffn, g_post_ffn,
                   f_w_up, f_conv_w, f_conv_b, f_w_down):
    sm = jax.nn.softmax(b_lb.astype(jnp.float32), axis=1)
    lb = jnp.cumsum(sm, axis=1) - sm[:, :1]
    for i in range(DEPTH):
        j = i // N_MIXERS
        h = _rmsnorm(x, g_pre_mix[i])
        mem_n = _rmsnorm(mem, g_mem[i])
        if i % N_MIXERS == 0:
            p = h @ a_w_in[j]
            mix = _mlstm_mixer(p[..., :A_REC], a_b_gate[j], a_g_head[j])
            w_out = a_w_out[j]
        else:
            p = h @ b_w_in[j]
            mix = _hgrn2_mixer(p[..., :B_REC], lb[0, j], lb[1, j], b_g_head[j])
            w_out = b_w_out[j]
        xa = _cross_attention(p[..., -X_W:], mem_n, x_w_kv[i])
        y = jnp.concatenate([mix, xa], axis=-1) @ w_out
        x = x + _rmsnorm(y, g_post_mix[i])
        f = _conv_glu_ffn(_rmsnorm(x, g_pre_ffn[i]), f_w_up[i], f_conv_w[i], f_conv_b[i], f_w_down[i])
        x = x + _rmsnorm(f, g_post_ffn[i])
    return x


def setup_inputs(seed: int = 0) -> dict:
    key = jax.random.key(seed)
    ks = jax.random.split(key, 24)
    f32 = jnp.float32

    def nrm(k, shape, scale):
        return jax.random.normal(k, shape, f32) * scale

    def gain(k, shape):
        return 1.0 + 0.05 * jax.random.normal(k, shape, f32)

    gate_base = jnp.concatenate([jnp.full((A_HEADS,), -1.0, f32), jnp.full((A_HEADS,), 3.0, f32),
                                 jnp.full((A_HEADS,), -1.0, f32), jnp.full((A_HEADS,), 3.0, f32)])
    conv_base = jnp.array([0.0, 1.0, 0.0], f32)[None, :, None]
    return {
        "x_prompt": nrm(ks[0], (BATCH, SEQ, D_MODEL), 1.0),
        "x_sample": nrm(ks[1], (DEC_BATCH, DEC_SEQ, D_MODEL), 1.0),
        "mem_prompt": nrm(ks[2], (BATCH, N_MEM, D_MODEL), 1.0),
        "mem_sample": nrm(ks[3], (DEC_BATCH, N_MEM, D_MODEL), 1.0),
        "g_pre_mix": gain(ks[4], (DEPTH, D_MODEL)),
        "g_post_mix": gain(ks[5], (DEPTH, D_MODEL)),
        "g_mem": gain(ks[6], (DEPTH, D_MODEL)),
        "a_w_in": nrm(ks[7], (N_A_LAYERS, D_MODEL, A_PROJ), D_MODEL ** -0.5),
        "a_b_gate": gate_base + 0.1 * nrm(ks[8], (N_A_LAYERS, 4 * A_HEADS), 1.0),
        "a_g_head": gain(ks[9], (N_A_LAYERS, A_HEADS * A_V)),
        "a_w_out": nrm(ks[10], (N_A_LAYERS, A_OUT, D_MODEL), A_OUT ** -0.5),
        "b_w_in": nrm(ks[11], (N_B_LAYERS, D_MODEL, B_PROJ), D_MODEL ** -0.5),
        "b_lb": nrm(ks[12], (2, N_B_LAYERS, B_HEADS * B_F), 0.5),
        "b_g_head": gain(ks[13], (N_B_LAYERS, B_HEADS * B_V)),
        "b_w_out": nrm(ks[14], (N_B_LAYERS, B_OUT, D_MODEL), B_OUT ** -0.5),
        "x_w_kv": nrm(ks[15], (DEPTH, D_MODEL, 2 * X_W), D_MODEL ** -0.5),
        "g_pre_ffn": gain(ks[16], (DEPTH, D_MODEL)),
        "g_post_ffn": gain(ks[17], (DEPTH, D_MODEL)),
        "f_w_up": nrm(ks[18], (DEPTH, D_MODEL, 2 * D_FF), D_MODEL ** -0.5),
        "f_conv_w": conv_base + 0.3 * nrm(ks[19], (DEPTH, CONV_W, 2 * D_FF), 1.0),
        "f_conv_b": nrm(ks[20], (DEPTH, 2 * D_FF), 0.02),
        "f_w_down": nrm(ks[21], (DEPTH, D_FF, D_MODEL), D_FF ** -0.5),
    }


def reference(x_prompt, x_sample, mem_prompt, mem_sample, g_pre_mix, g_post_mix, g_mem,
              a_w_in, a_b_gate, a_g_head, a_w_out, b_w_in, b_lb, b_g_head, b_w_out, x_w_kv,
              g_pre_ffn, g_post_ffn, f_w_up, f_conv_w, f_conv_b, f_w_down):
    y_prompt = _encoder_stack(x_prompt, mem_prompt, g_pre_mix, g_post_mix, g_mem, a_w_in, a_b_gate,
                              a_g_head, a_w_out, b_w_in, b_lb, b_g_head, b_w_out, x_w_kv,
                              g_pre_ffn, g_post_ffn, f_w_up, f_conv_w, f_conv_b, f_w_down)
    y_sample = _encoder_stack(x_sample, mem_sample, g_pre_mix, g_post_mix, g_mem, a_w_in, a_b_gate,
                              a_g_head, a_w_out, b_w_in, b_lb, b_g_head, b_w_out, x_w_kv,
                              g_pre_ffn, g_post_ffn, f_w_up, f_conv_w, f_conv_b, f_w_down)
    return (y_prompt, y_sample)
```

```python
import functools

import jax
import jax.numpy as jnp
from jax import lax
from jax.experimental import pallas as pl
from jax.experimental.pallas import tpu as pltpu

F32 = jnp.float32
BF16 = jnp.bfloat16
EPS = 1e-6
HIGHEST = lax.Precision.HIGHEST

D_MODEL = 1024
N_MEM = 256
X_HEADS = 4
X_HD = 128
X_W = X_HEADS * X_HD
A_HEADS = 4
A_QK = 128
A_V = 256
B_HEADS = 8
B_F = 128
B_V = 128
CONV_W = 3

P_MAIN = 3 * D_MODEL + X_W
GATE_COL_BLOCK = 2
XQ_COL_BLOCK = 6
A_GATE_LANES = 128

VMEM_LIMIT_BYTES = 56 * 1024 * 1024

ROW_TILE = 512
MIX_ROWS = 512
A_CHUNK = 64
B_CHUNK = 32
B_DIAG = 8
FF_CHUNK = 256


def _params(sem):
    return pltpu.CompilerParams(dimension_semantics=sem, vmem_limit_bytes=VMEM_LIMIT_BYTES)


def _const_spec(shape):
    nd = len(shape)
    return pl.BlockSpec(shape, lambda *_: (0,) * nd, pipeline_mode=pl.Buffered(1))


def _rms(x, g):
    ms = jnp.mean(x * x, axis=-1, keepdims=True)
    return x * lax.rsqrt(ms + EPS) * g


def _sigmoid(x):
    return 1.0 / (1.0 + jnp.exp(-x))


def _log_sigmoid(x):
    return jnp.minimum(x, 0.0) - jnp.log(1.0 + jnp.exp(-jnp.abs(x)))


def _norm_proj_kernel(x_ref, g_ref, wa_ref, *rest, col_chunk):
    if len(rest) == 3:
        wb_ref, oa_ref, ob_ref = rest
    else:
        (oa_ref,) = rest
        wb_ref = ob_ref = None
    h = _rms(x_ref[...], g_ref[...]).astype(BF16)
    for w_ref, o_ref in ((wa_ref, oa_ref), (wb_ref, ob_ref)):
        if w_ref is None:
            continue
        n = w_ref.shape[1]
        for c0 in range(0, n, col_chunk):
            cw = min(col_chunk, n - c0)
            o_ref[:, c0:c0 + cw] = jnp.dot(h, w_ref[:, c0:c0 + cw],
                                           preferred_element_type=F32).astype(o_ref.dtype)


def _norm_proj(x2d, g, wa, wb=None):
    m, d = x2d.shape
    tm = min(ROW_TILE, m)
    in_specs = [pl.BlockSpec((tm, d), lambda i: (i, 0)), _const_spec((1, d)), _const_spec(wa.shape)]
    out_shape = [jax.ShapeDtypeStruct((m, wa.shape[1]), BF16)]
    out_specs = [pl.BlockSpec((tm, wa.shape[1]), lambda i: (i, 0))]
    args = [x2d, g.reshape(1, d), wa]
    if wb is not None:
        in_specs.append(_const_spec(wb.shape))
        out_shape.append(jax.ShapeDtypeStruct((m, wb.shape[1]), F32))
        out_specs.append(pl.BlockSpec((tm, wb.shape[1]), lambda i: (i, 0)))
        args.append(wb)
    out = pl.pallas_call(
        functools.partial(_norm_proj_kernel, col_chunk=512),
        grid=(m // tm,),
        in_specs=in_specs,
        out_specs=out_specs,
        out_shape=out_shape,
        compiler_params=_params(("parallel",)),
        name="norm_proj",
    )(*args)
    return out if wb is not None else out[0]


def _mlstm_kernel(qf_ref, kf_ref, vf_ref, gf_ref, qb_ref, kb_ref, vb_ref, gb_ref, bias_ref,
                  hf_ref, hb_ref, c_ref, n_ref, m_ref, *, chunk):
    L = chunk
    nblk = qf_ref.shape[0] // L
    scale = A_QK ** -0.5

    @pl.when(pl.program_id(1) == 0)
    def _():
        c_ref[...] = jnp.zeros_like(c_ref)
        n_ref[...] = jnp.zeros_like(n_ref)
        m_ref[...] = jnp.zeros_like(m_ref)

    ti = lax.broadcasted_iota(jnp.int32, (L, L), 0)
    si = lax.broadcasted_iota(jnp.int32, (L, L), 1)
    masks = (si <= ti, si >= ti)
    cums = (masks[0].astype(F32), masks[1].astype(F32))
    bias = bias_ref[...]
    views = ((qf_ref, kf_ref, vf_ref, gf_ref, hf_ref), (qb_ref, kb_ref, vb_ref, gb_ref, hb_ref))

    def body(i, carry):
        for d in range(2):
            q_ref, k_ref, v_ref, g_ref, h_ref = views[d]
            r0 = pl.multiple_of((i if d == 0 else nblk - 1 - i) * L, L)
            rows = pl.ds(r0, L)
            gpre = g_ref[rows, :] + bias
            cum = jnp.dot(cums[d], _log_sigmoid(gpre), precision=HIGHEST,
                          preferred_element_type=F32)
            gpre_t = gpre.T
            cum_t = cum.T
            for hd in range(A_HEADS):
                chain = d * A_HEADS + hd
                ci = d * 2 * A_HEADS + hd
                cf = ci + A_HEADS
                b_col = cum[:, cf:cf + 1]
                b_row = cum_t[cf:cf + 1, :]
                li_col = gpre[:, ci:ci + 1]
                li_row = gpre_t[ci:ci + 1, :]
                bl = cum[L - 1:L, cf:cf + 1] if d == 0 else cum[0:1, cf:cf + 1]
                m_old = m_ref[chain][:, 0:1]
                dm = jnp.where(masks[d], b_col - b_row + li_row, -jnp.inf)
                inter = b_col + m_old
                mt = jnp.maximum(inter, jnp.max(dm, axis=1, keepdims=True))
                q = q_ref[rows, hd * A_QK:(hd + 1) * A_QK]
                k = k_ref[rows, hd * A_QK:(hd + 1) * A_QK]
                v = v_ref[rows, hd * A_V:(hd + 1) * A_V]
                qk = lax.dot_general(q, k, (((1,), (1,)), ((), ())), preferred_element_type=F32)
                s = qk * (scale * jnp.exp(dm - mt))
                ei = scale * jnp.exp(inter - mt)
                c_old = c_ref[chain]
                n_old = n_ref[chain]
                num = (jnp.dot(s.astype(BF16), v, preferred_element_type=F32)
                       + ei * jnp.dot(q, c_old.astype(BF16), preferred_element_type=F32))
                den = (jnp.sum(s, axis=1, keepdims=True)
                       + ei * jnp.sum(q.astype(F32) * n_old, axis=1, keepdims=True))
                h_ref[rows, hd * A_V:(hd + 1) * A_V] = num / jnp.maximum(jnp.abs(den), jnp.exp(-mt))
                g_col = bl + li_col - b_col
                m_new = jnp.maximum(bl + m_old, jnp.max(g_col, axis=0, keepdims=True))
                kw = jnp.exp(g_col - m_new) * k.astype(F32)
                dec = jnp.exp(bl + m_old - m_new)
                c_ref[chain] = dec * c_old + jnp.dot(kw.T.astype(BF16), v, preferred_element_type=F32)
                n_ref[chain] = dec * n_old + jnp.sum(kw, axis=0, keepdims=True)
                m_ref[chain] = jnp.broadcast_to(m_new, (1, 128))
        return carry

    lax.fori_loop(0, nblk, body, 0)


def _mlstm_mixer(p_main, gates, bias, batch, seq):
    m = batch * seq
    rows = min(MIX_ROWS, seq)
    nb = seq // rows
    fwd = lambda b, j: b * nb + j
    bwd = lambda b, j: b * nb + nb - 1 - j

    def specs(row):
        return [pl.BlockSpec((rows, 512), lambda b, j: (row(b, j), 0)),
                pl.BlockSpec((rows, 512), lambda b, j: (row(b, j), 1)),
                pl.BlockSpec((rows, 1024), lambda b, j: (row(b, j), 1)),
                pl.BlockSpec((rows, A_GATE_LANES), lambda b, j: (row(b, j), 0))]

    n_chain = 2 * A_HEADS
    return pl.pallas_call(
        functools.partial(_mlstm_kernel, chunk=A_CHUNK),
        grid=(batch, nb),
        in_specs=specs(fwd) + specs(bwd) + [_const_spec((1, A_GATE_LANES))],
        out_specs=[pl.BlockSpec((rows, D_MODEL), lambda b, j: (fwd(b, j), 0)),
                   pl.BlockSpec((rows, D_MODEL), lambda b, j: (bwd(b, j), 0))],
        out_shape=[jax.ShapeDtypeStruct((m, D_MODEL), F32)] * 2,
        scratch_shapes=[pltpu.VMEM((n_chain, A_QK, A_V), F32),
                        pltpu.VMEM((n_chain, 1, A_QK), F32),
                        pltpu.VMEM((n_chain, 1, 128), F32)],
        compiler_params=_params(("parallel", "arbitrary")),
        name="mlstm_mixer",
    )(p_main, p_main, p_main, gates, p_main, p_main, p_main, gates, bias)


def _hgrn_kernel(qf_ref, vf_ref, zf_ref, qb_ref, vb_ref, zb_ref, lb_ref, of_ref, ob_ref, st_ref,
                 *, chunk, layer):
    L = chunk
    nblk = qf_ref.shape[0] // L
    W = B_HEADS * B_F

    @pl.when(pl.program_id(1) == 0)
    def _():
        st_ref[...] = jnp.zeros_like(st_ref)

    ti = lax.broadcasted_iota(jnp.int32, (L, L), 0)
    si = lax.broadcasted_iota(jnp.int32, (L, L), 1)
    cums = ((si <= ti).astype(F32), (si >= ti).astype(F32))
    row_l = lax.broadcasted_iota(jnp.int32, (L, 1), 0)
    row_8 = lax.broadcasted_iota(jnp.int32, (B_DIAG, 1), 0)
    levels = []
    c = B_DIAG
    while c < L:
        levels.append(c)
        c *= 2

    lbs = []
    for d in range(2):
        p = lb_ref[d]
        e = jnp.exp(p - jnp.max(p, axis=0, keepdims=True))
        sm = e / jnp.sum(e, axis=0, keepdims=True)
        lb = jnp.zeros((1, W), F32)
        for r in range(1, layer + 1):
            lb = lb + sm[r:r + 1, :]
        lbs.append(lb)
    views = ((qf_ref, vf_ref, zf_ref, of_ref), (qb_ref, vb_ref, zb_ref, ob_ref))

    def body(i, carry):
        for d in range(2):
            q_ref, v_ref, z_ref, o_ref = views[d]
            lb = lbs[d]
            r0 = pl.multiple_of((i if d == 0 else nblk - 1 - i) * L, L)
            rows = pl.ds(r0, L)
            z = z_ref[rows, :]
            f = lb + (1.0 - lb) * _sigmoid(z)
            kk = (1.0 - lb) * _sigmoid(-z)
            b = jnp.dot(cums[d], jnp.log(f), precision=HIGHEST, preferred_element_type=F32)
            bl = b[L - 1:L, :] if d == 0 else b[0:1, :]
            q = q_ref[rows, :].astype(F32)
            v_bf = v_ref[rows, :]
            v = v_bf.astype(F32)
            qe = (q * jnp.exp(b)).astype(BF16)
            ke = (kk * jnp.exp(bl - b))

            qts, kss, pair_masks = [], [], []
            for c in levels:
                blk = row_l // c
                is_t = (blk % 2 == 1) if d == 0 else (blk % 2 == 0)
                refs = []
                for p0 in range(0, L, 2 * c):
                    ridx = p0 + c - 1 if d == 0 else p0 + c
                    refs.append(jnp.broadcast_to(b[ridx:ridx + 1, :], (2 * c, W)))
                ref = refs[0] if len(refs) == 1 else jnp.concatenate(refs, axis=0)
                e = jnp.exp(jnp.where(is_t, b - ref, ref - b))
                qts.append(jnp.where(is_t, q * e, 0.0).astype(BF16))
                kss.append(jnp.where(is_t, 0.0, kk * e).astype(BF16))
                pair_masks.append((ti // (2 * c)) == (si // (2 * c)))

            diag = [[None] * (L // B_DIAG) for _ in range(B_HEADS)]
            for g in range(L // B_DIAG):
                sl = slice(g * B_DIAG, (g + 1) * B_DIAG)
                f_g, q_g, k_g, v_g = f[sl], q[sl], kk[sl], v[sl]
                y = q_g * k_g
                acc = [jnp.sum(y[:, h * B_F:(h + 1) * B_F], axis=1, keepdims=True) * v_g[:, h * B_V:(h + 1) * B_V]
                       for h in range(B_HEADS)]
                dprod = None
                for delta in range(1, B_DIAG):
                    if d == 0:
                        f_s = f_g if delta == 1 else pltpu.roll(f_g, delta - 1, 0)
                        k_s = pltpu.roll(k_g, delta, 0)
                        v_s = pltpu.roll(v_g, delta, 0)
                        valid = row_8 >= delta
                    else:
                        f_s = f_g if delta == 1 else pltpu.roll(f_g, B_DIAG - (delta - 1), 0)
                        k_s = pltpu.roll(k_g, B_DIAG - delta, 0)
                        v_s = pltpu.roll(v_g, B_DIAG - delta, 0)
                        valid = row_8 + delta <= B_DIAG - 1
                    dprod = f_s if dprod is None else dprod * f_s
                    y = jnp.where(valid, q_g * k_s * dprod, 0.0)
                    for h in range(B_HEADS):
                        a = jnp.sum(y[:, h * B_F:(h + 1) * B_F], axis=1, keepdims=True)
                        acc[h] = acc[h] + a * v_s[:, h * B_V:(h + 1) * B_V]
                for h in range(B_HEADS):
                    diag[h][g] = acc[h]

            for h in range(B_HEADS):
                chain = d * B_HEADS + h
                fs = slice(h * B_F, (h + 1) * B_F)
                vs = slice(h * B_V, (h + 1) * B_V)
                st = st_ref[chain]
                o = lax.dot_general(qe[:, fs], st.astype(BF16), (((1,), (1,)), ((), ())),
                                    preferred_element_type=F32)
                a_off = None
                for li in range(len(levels)):
                    a = lax.dot_general(qts[li][:, fs], kss[li][:, fs], (((1,), (1,)), ((), ())),
                                        preferred_element_type=F32)
                    a = jnp.where(pair_masks[li], a, 0.0)
                    a_off = a if a_off is None else a_off + a
                if a_off is not None:
                    o = o + jnp.dot(a_off.astype(BF16), v_bf[:, vs], preferred_element_type=F32)
                o_ref[rows, vs] = o + jnp.concatenate(diag[h], axis=0)
                st_ref[chain] = st * jnp.exp(bl[:, fs]) + jnp.dot(
                    v[:, vs].T.astype(BF16), ke[:, fs].astype(BF16), preferred_element_type=F32)
        return carry

    lax.fori_loop(0, nblk, body, 0)


def _hgrn_mixer(p_main, zgates, b_lb, layer, batch, seq):
    m = batch * seq
    rows = min(MIX_ROWS, seq)
    nb = seq // rows
    fwd = lambda b, j: b * nb + j
    bwd = lambda b, j: b * nb + nb - 1 - j

    def specs(row, zcol):
        return [pl.BlockSpec((rows, 1024), lambda b, j: (row(b, j), 0)),
                pl.BlockSpec((rows, 1024), lambda b, j: (row(b, j), 1)),
                pl.BlockSpec((rows, 1024), lambda b, j: (row(b, j), zcol))]

    return pl.pallas_call(
        functools.partial(_hgrn_kernel, chunk=B_CHUNK, layer=layer),
        grid=(batch, nb),
        in_specs=specs(fwd, 0) + specs(bwd, 1) + [_const_spec(b_lb.shape)],
        out_specs=[pl.BlockSpec((rows, D_MODEL), lambda b, j: (fwd(b, j), 0)),
                   pl.BlockSpec((rows, D_MODEL), lambda b, j: (bwd(b, j), 0))],
        out_shape=[jax.ShapeDtypeStruct((m, D_MODEL), F32)] * 2,
        scratch_shapes=[pltpu.VMEM((2 * B_HEADS, B_V, B_F), F32)],
        compiler_params=_params(("parallel", "arbitrary")),
        name="hgrn_mixer",
    )(p_main, p_main, zgates, p_main, p_main, zgates, b_lb)


def _mix_out_kernel(hf_ref, hb_ref, gate_ref, xq_ref, kv_ref, wo_ref, gh_ref, gp_ref, x_ref, o_ref,
                    *, head_dim, silu_gate):
    h = hf_ref[...] + hb_ref[...]
    parts = []
    for c0 in range(0, D_MODEL, head_dim):
        hh = h[:, c0:c0 + head_dim]
        parts.append(hh * lax.rsqrt(jnp.mean(hh * hh, axis=-1, keepdims=True) + EPS))
    gate = gate_ref[...].astype(F32)
    gate = gate * _sigmoid(gate) if silu_gate else _sigmoid(gate)
    mix = (jnp.concatenate(parts, axis=-1) * gh_ref[...] * gate).astype(BF16)
    y = jnp.dot(mix, wo_ref[0:D_MODEL, :], preferred_element_type=F32)

    xa = []
    for hd in range(X_HEADS):
        cs = slice(hd * X_HD, (hd + 1) * X_HD)
        s = lax.dot_general(xq_ref[:, cs], kv_ref[:, cs], (((1,), (1,)), ((), ())),
                            preferred_element_type=F32) * (X_HD ** -0.5)
        e = jnp.exp(s - jnp.max(s, axis=-1, keepdims=True))
        pv = jnp.dot(e.astype(BF16), kv_ref[:, X_W + hd * X_HD:X_W + (hd + 1) * X_HD],
                     preferred_element_type=F32)
        xa.append((pv / jnp.sum(e, axis=-1, keepdims=True)).astype(BF16))
    y = y + jnp.dot(jnp.concatenate(xa, axis=-1), wo_ref[D_MODEL:D_MODEL + X_W, :], preferred_element_type=F32)
    o_ref[...] = x_ref[...] + _rms(y, gp_ref[...])


def _mix_out(hf, hb, p_main, kv, w_out, g_head, g_post, x2d, batch, seq, head_dim, silu_gate):
    m = batch * seq
    tm = min(ROW_TILE, seq)
    nb = seq // tm
    row = lambda b, j: b * nb + j
    tile = pl.BlockSpec((tm, D_MODEL), lambda b, j: (row(b, j), 0))
    return pl.pallas_call(
        functools.partial(_mix_out_kernel, head_dim=head_dim, silu_gate=silu_gate),
        grid=(batch, nb),
        in_specs=[tile, tile,
                  pl.BlockSpec((tm, D_MODEL), lambda b, j: (row(b, j), GATE_COL_BLOCK)),
                  pl.BlockSpec((tm, X_W), lambda b, j: (row(b, j), XQ_COL_BLOCK)),
                  pl.BlockSpec((N_MEM, 2 * X_W), lambda b, j: (b, 0)),
                  _const_spec(w_out.shape), _const_spec((1, D_MODEL)), _const_spec((1, D_MODEL)),
                  tile],
        out_specs=tile,
        out_shape=jax.ShapeDtypeStruct((m, D_MODEL), F32),
        compiler_params=_params(("parallel", "parallel")),
        name="mix_out",
    )(hf, hb, p_main, p_main, kv, w_out, g_head.reshape(1, D_MODEL), g_post.reshape(1, D_MODEL), x2d)


HALO = 16


def _ffn_kernel(x_ref, xp_ref, xn_ref, gpre_ref, wup_ref, cw_ref, cb_ref, wdn_ref, gpost_ref, o_ref,
                hext_ref, acc_ref, *, d_ff, fc):
    tm = x_ref.shape[0]
    j = pl.program_id(1)
    gpre = gpre_ref[...]
    x = x_ref[...]
    hext_ref[0:tm, :] = _rms(x, gpre).astype(BF16)
    hext_ref[tm:tm + HALO, :] = _rms(xp_ref[...], gpre).astype(BF16)
    hext_ref[tm + HALO:tm + 2 * HALO, :] = _rms(xn_ref[...], gpre).astype(BF16)
    has_prev = (j > 0).astype(F32)
    has_next = (j < pl.num_programs(1) - 1).astype(F32)
    row = lax.broadcasted_iota(jnp.int32, (tm, 1), 0)
    first, last = row == 0, row == tm - 1
    acc_ref[...] = jnp.zeros_like(acc_ref)
    for c0 in range(0, d_ff, fc):
        conv = []
        for base in (c0, d_ff + c0):
            cs = slice(base, base + fc)
            ue = jnp.dot(hext_ref[...], wup_ref[:, cs], preferred_element_type=F32)
            u = ue[0:tm]
            u_prev = jnp.where(first, ue[tm + HALO - 1:tm + HALO] * has_prev, pltpu.roll(u, 1, 0))
            u_next = jnp.where(last, ue[tm + HALO:tm + HALO + 1] * has_next, pltpu.roll(u, tm - 1, 0))
            conv.append(u_prev * cw_ref[0:1, cs] + u * cw_ref[1:2, cs] + u_next * cw_ref[2:3, cs] + cb_ref[:, cs])
        a, g = conv
        act = (g * _sigmoid(g) * a).astype(BF16)
        acc_ref[...] += jnp.dot(act, wdn_ref[c0:c0 + fc, :], preferred_element_type=F32)
    o_ref[...] = x + _rms(acc_ref[...], gpost_ref[...])


def _ffn(x2d, g_pre, w_up, conv_w, conv_b, w_down, g_post, batch, seq):
    m = batch * seq
    tm = min(ROW_TILE, seq)
    nb = seq // tm
    hb = tm // HALO
    last_halo = m // HALO - 1
    d_ff = w_down.shape[0]
    tile = pl.BlockSpec((tm, D_MODEL), lambda b, j: (b * nb + j, 0))
    return pl.pallas_call(
        functools.partial(_ffn_kernel, d_ff=d_ff, fc=FF_CHUNK),
        grid=(batch, nb),
        in_specs=[tile,
                  pl.BlockSpec((HALO, D_MODEL), lambda b, j: (jnp.maximum((b * nb + j) * hb - 1, 0), 0)),
                  pl.BlockSpec((HALO, D_MODEL), lambda b, j: (jnp.minimum((b * nb + j + 1) * hb, last_halo), 0)),
                  _const_spec((1, D_MODEL)), _const_spec(w_up.shape), _const_spec(conv_w.shape),
                  _const_spec((1, 2 * d_ff)), _const_spec(w_down.shape), _const_spec((1, D_MODEL))],
        out_specs=tile,
        out_shape=jax.ShapeDtypeStruct((m, D_MODEL), F32),
        scratch_shapes=[pltpu.VMEM((tm + 2 * HALO, D_MODEL), BF16), pltpu.VMEM((tm, D_MODEL), F32)],
        compiler_params=_params(("parallel", "parallel")),
        name="ffn",
    )(x2d, x2d, x2d, g_pre.reshape(1, D_MODEL), w_up, conv_w, conv_b.reshape(1, 2 * d_ff), w_down,
      g_post.reshape(1, D_MODEL))


def _encoder_stack(x, mem, g_pre_mix, g_post_mix, g_mem, a_w_main, a_w_gate, a_bias, a_g_head, a_w_out,
                   b_w_main, b_w_gate, b_lb, b_g_head, b_w_out, x_w_kv, g_pre_ffn, g_post_ffn,
                   f_w_up, f_conv_w, f_conv_b, f_w_down):
    batch, seq, d = x.shape
    x2d = x.reshape(batch * seq, d)
    mem2d = mem.reshape(batch * N_MEM, d)
    depth = g_pre_mix.shape[0]
    for i in range(depth):
        j = i // 2
        kv = _norm_proj(mem2d, g_mem[i], x_w_kv[i])
        if i % 2 == 0:
            p_main, gates = _norm_proj(x2d, g_pre_mix[i], a_w_main[j], a_w_gate[j])
            hf, hb = _mlstm_mixer(p_main, gates, a_bias[j], batch, seq)
            x2d = _mix_out(hf, hb, p_main, kv, a_w_out[j], a_g_head[j], g_post_mix[i], x2d, batch, seq,
                           head_dim=A_V, silu_gate=False)
        else:
            p_main, zgates = _norm_proj(x2d, g_pre_mix[i], b_w_main[j], b_w_gate[j])
            hf, hb = _hgrn_mixer(p_main, zgates, b_lb, j, batch, seq)
            x2d = _mix_out(hf, hb, p_main, kv, b_w_out[j], b_g_head[j], g_post_mix[i], x2d, batch, seq,
                           head_dim=B_V, silu_gate=True)
        x2d = _ffn(x2d, g_pre_ffn[i], f_w_up[i], f_conv_w[i], f_conv_b[i], f_w_down[i], g_post_ffn[i],
                   batch, seq)
    return x2d.reshape(batch, seq, d)


def kernel(x_prompt, x_sample, mem_prompt, mem_sample, g_pre_mix, g_post_mix, g_mem, a_w_in, a_b_gate, a_g_head, a_w_out, b_w_in, b_lb, b_g_head, b_w_out, x_w_kv, g_pre_ffn, g_post_ffn, f_w_up, f_conv_w, f_conv_b, f_w_down):
    a_rec = 2 * A_HEADS * A_QK + 2 * A_HEADS * A_V
    n_gate = 4 * A_HEADS
    a_w_main = jnp.concatenate([a_w_in[:, :, :a_rec], a_w_in[:, :, a_rec + n_gate:]], axis=-1).astype(BF16)
    a_w_gate = jnp.pad(a_w_in[:, :, a_rec:a_rec + n_gate], ((0, 0), (0, 0), (0, A_GATE_LANES - n_gate))).astype(BF16)
    a_bias = jnp.pad(a_b_gate, ((0, 0), (0, A_GATE_LANES - n_gate))).reshape(-1, 1, A_GATE_LANES)
    w = B_HEADS * B_F
    b_w_main = jnp.concatenate([b_w_in[:, :, :2 * w], b_w_in[:, :, 4 * w:]], axis=-1).astype(BF16)
    b_w_gate = b_w_in[:, :, 2 * w:4 * w].astype(BF16)
    shared = (g_pre_mix, g_post_mix, g_mem, a_w_main, a_w_gate, a_bias, a_g_head, a_w_out.astype(BF16),
              b_w_main, b_w_gate, b_lb, b_g_head, b_w_out.astype(BF16), x_w_kv.astype(BF16),
              g_pre_ffn, g_post_ffn, f_w_up.astype(BF16), f_conv_w, f_conv_b, f_w_down.astype(BF16))
    return (_encoder_stack(x_prompt, mem_prompt, *shared), _encoder_stack(x_sample, mem_sample, *shared))
```

```python
import functools

import jax
import jax.numpy as jnp
from jax import lax
from jax.experimental import pallas as pl
from jax.experimental.pallas import tpu as pltpu

F32 = jnp.float32
BF16 = jnp.bfloat16
EPS = 1e-6

D_MODEL = 1024
N_MEM = 256
X_HEADS = 4
X_HD = 128
X_W = X_HEADS * X_HD
A_HEADS = 4
A_QK = 128
A_V = 256
B_HEADS = 8
B_F = 128
B_V = 128

GATE_COL_BLOCK = 2
XQ_COL_BLOCK = 6
A_GATE_LANES = 128
SUBLANES = 8

VMEM_LIMIT_BYTES = 56 * 1024 * 1024

ROW_TILE = 512
MIX_ROWS = 512
A_CHUNK = 128
B_CHUNK = 64
FF_CHUNK = 256
FF_GROUPS = 2


def _params(sem):
    return pltpu.CompilerParams(dimension_semantics=sem, vmem_limit_bytes=VMEM_LIMIT_BYTES)


def _const_spec(shape):
    nd = len(shape)
    return pl.BlockSpec(shape, lambda *_: (0,) * nd, pipeline_mode=pl.Buffered(1))


def _rms(x, g):
    ms = jnp.mean(x * x, axis=-1, keepdims=True)
    return x * lax.rsqrt(ms + EPS) * g


def _sigmoid(x):
    return 1.0 / (1.0 + jnp.exp(-x))


def _log_sigmoid(x):
    return jnp.minimum(x, 0.0) - jnp.log(1.0 + jnp.exp(-jnp.abs(x)))


def _dot(a, b):
    return jnp.dot(a, b, preferred_element_type=F32)


def _dot_nt(a, b):
    return lax.dot_general(a, b, (((1,), (1,)), ((), ())), preferred_element_type=F32)


def _cumsum_rows(tri_bf, x):
    hi = x.astype(BF16)
    r1 = x - hi.astype(F32)
    mid = r1.astype(BF16)
    lo = (r1 - mid.astype(F32)).astype(BF16)
    return _dot(tri_bf, hi) + _dot(tri_bf, mid) + _dot(tri_bf, lo)


def _norm_proj_kernel(x_ref, g_ref, wa_ref, *rest, col_chunk):
    if len(rest) == 3:
        wb_ref, oa_ref, ob_ref = rest
    else:
        (oa_ref,) = rest
        wb_ref = ob_ref = None
    h = _rms(x_ref[...], g_ref[...]).astype(BF16)
    for w_ref, o_ref in ((wa_ref, oa_ref), (wb_ref, ob_ref)):
        if w_ref is None:
            continue
        n = w_ref.shape[1]
        for c0 in range(0, n, col_chunk):
            cw = min(col_chunk, n - c0)
            o_ref[:, c0:c0 + cw] = _dot(h, w_ref[:, c0:c0 + cw]).astype(o_ref.dtype)


def _norm_proj(x2d, g, wa, wb=None):
    m, d = x2d.shape
    tm = min(ROW_TILE, m)
    in_specs = [pl.BlockSpec((tm, d), lambda i: (i, 0)), _const_spec((1, d)), _const_spec(wa.shape)]
    out_shape = [jax.ShapeDtypeStruct((m, wa.shape[1]), BF16)]
    out_specs = [pl.BlockSpec((tm, wa.shape[1]), lambda i: (i, 0))]
    args = [x2d, g.reshape(1, d), wa]
    if wb is not None:
        in_specs.append(_const_spec(wb.shape))
        out_shape.append(jax.ShapeDtypeStruct((m, wb.shape[1]), F32))
        out_specs.append(pl.BlockSpec((tm, wb.shape[1]), lambda i: (i, 0)))
        args.append(wb)
    out = pl.pallas_call(
        functools.partial(_norm_proj_kernel, col_chunk=512),
        grid=(m // tm,),
        in_specs=in_specs,
        out_specs=out_specs,
        out_shape=out_shape,
        compiler_params=_params(("parallel",)),
        name="norm_proj",
    )(*args)
    return out if wb is not None else out[0]


def _mlstm_kernel(qf_ref, kf_ref, vf_ref, gf_ref, qb_ref, kb_ref, vb_ref, gb_ref, bias_ref,
                  hf_ref, hb_ref, c_ref, n_ref, m_ref, *, chunk):
    L = chunk
    nch = qf_ref.shape[0] // L
    scale = A_QK ** -0.5

    @pl.when(pl.program_id(1) == 0)
    def _():
        c_ref[...] = jnp.zeros_like(c_ref)
        n_ref[...] = jnp.zeros_like(n_ref)
        m_ref[...] = jnp.zeros_like(m_ref)

    ti = lax.broadcasted_iota(jnp.int32, (L, L), 0)
    si = lax.broadcasted_iota(jnp.int32, (L, L), 1)
    masks = (si <= ti, si >= ti)
    tris = (masks[0].astype(BF16), masks[1].astype(BF16))
    bias = bias_ref[...]
    views = ((qf_ref, kf_ref, vf_ref, gf_ref, hf_ref), (qb_ref, kb_ref, vb_ref, gb_ref, hb_ref))

    for idx in range(nch):
        for d in range(2):
            q_ref, k_ref, v_ref, g_ref, h_ref = views[d]
            c0 = (idx if d == 0 else nch - 1 - idx) * L
            rows = slice(c0, c0 + L)
            gpre = g_ref[rows, :] + bias
            cum = _cumsum_rows(tris[d], _log_sigmoid(gpre))
            gpre_t = gpre.T
            cum_t = cum.T
            for hd in range(A_HEADS):
                chain = d * A_HEADS + hd
                ci = d * 2 * A_HEADS + hd
                cf = ci + A_HEADS
                b_col = cum[:, cf:cf + 1]
                b_row = cum_t[cf:cf + 1, :]
                li_col = gpre[:, ci:ci + 1]
                li_row = gpre_t[ci:ci + 1, :]
                bl = cum[L - 1:L, cf:cf + 1] if d == 0 else cum[0:1, cf:cf + 1]
                q = q_ref[rows, hd * A_QK:(hd + 1) * A_QK]
                k = k_ref[rows, hd * A_QK:(hd + 1) * A_QK]
                v = v_ref[rows, hd * A_V:(hd + 1) * A_V]
                dm = jnp.where(masks[d], b_col - b_row + li_row, -jnp.inf)
                m_loc = jnp.max(dm, axis=1, keepdims=True)
                s = _dot_nt(q, k) * jnp.exp(dm - m_loc)
                num_loc = _dot(s.astype(BF16), v)
                den_loc = jnp.sum(s, axis=1, keepdims=True)
                g_col = bl + li_col - b_col
                g_max = jnp.max(g_col, axis=0, keepdims=True)
                kw = jnp.exp(g_col - g_max) * k.astype(F32)
                dc = _dot(kw.T.astype(BF16), v)
                dn = jnp.sum(kw, axis=0, keepdims=True)
                m_old = m_ref[chain][:, 0:1]
                c_old = c_ref[chain]
                n_old = n_ref[chain]
                inter = b_col + m_old
                mt = jnp.maximum(inter, m_loc)
                a_loc = scale * jnp.exp(m_loc - mt)
                a_int = scale * jnp.exp(inter - mt)
                num = a_loc * num_loc + a_int * _dot(q, c_old.astype(BF16))
                den = a_loc * den_loc + a_int * jnp.sum(q.astype(F32) * n_old, axis=1, keepdims=True)
                h_ref[rows, hd * A_V:(hd + 1) * A_V] = num / jnp.maximum(jnp.abs(den), jnp.exp(-mt))
                m_new = jnp.maximum(bl + m_old, g_max)
                dec = jnp.exp(bl + m_old - m_new)
                inc = jnp.exp(g_max - m_new)
                c_ref[chain] = dec * c_old + inc * dc
                n_ref[chain] = dec * n_old + inc * dn
                m_ref[chain] = jnp.broadcast_to(m_new, (1, 128))


def _mlstm_mixer(p_main, gates, bias, batch, seq):
    m = batch * seq
    rows = min(MIX_ROWS, seq)
    nb = seq // rows
    fwd = lambda b, j: b * nb + j
    bwd = lambda b, j: b * nb + nb - 1 - j

    def specs(row):
        return [pl.BlockSpec((rows, 512), lambda b, j: (row(b, j), 0)),
                pl.BlockSpec((rows, 512), lambda b, j: (row(b, j), 1)),
                pl.BlockSpec((rows, 1024), lambda b, j: (row(b, j), 1)),
                pl.BlockSpec((rows, A_GATE_LANES), lambda b, j: (row(b, j), 0))]

    n_chain = 2 * A_HEADS
    return pl.pallas_call(
        functools.partial(_mlstm_kernel, chunk=A_CHUNK),
        grid=(batch, nb),
        in_specs=specs(fwd) + specs(bwd) + [_const_spec((1, A_GATE_LANES))],
        out_specs=[pl.BlockSpec((rows, D_MODEL), lambda b, j: (fwd(b, j), 0)),
                   pl.BlockSpec((rows, D_MODEL), lambda b, j: (bwd(b, j), 0))],
        out_shape=[jax.ShapeDtypeStruct((m, D_MODEL), F32)] * 2,
        scratch_shapes=[pltpu.VMEM((n_chain, A_QK, A_V), F32),
                        pltpu.VMEM((n_chain, 1, A_QK), F32),
                        pltpu.VMEM((n_chain, 1, 128), F32)],
        compiler_params=_params(("parallel", "arbitrary")),
        name="mlstm_mixer",
    )(p_main, p_main, p_main, gates, p_main, p_main, p_main, gates, bias)


def _hgrn_kernel(qf_ref, vf_ref, zf_ref, qb_ref, vb_ref, zb_ref, lb_ref, of_ref, ob_ref, st_ref, b_scr,
                 *, chunk, layer):
    L = chunk
    nch = qf_ref.shape[0] // L
    W = B_HEADS * B_F

    @pl.when(pl.program_id(1) == 0)
    def _():
        st_ref[...] = jnp.zeros_like(st_ref)

    ti = lax.broadcasted_iota(jnp.int32, (L, L), 0)
    si = lax.broadcasted_iota(jnp.int32, (L, L), 1)
    tris = ((si <= ti).astype(BF16), (si >= ti).astype(BF16))
    eye = ti == si
    row8 = lax.broadcasted_iota(jnp.int32, (SUBLANES, 1), 0)
    log2s = list(range(L.bit_length() - 2, -1, -1))

    def level_mask(k, d):
        same_pair = lax.shift_right_logical(ti, k + 1) == lax.shift_right_logical(si, k + 1)
        t_hi = (lax.shift_right_logical(ti, k) & 1) == (1 if d == 0 else 0)
        s_lo = (lax.shift_right_logical(si, k) & 1) == (0 if d == 0 else 1)
        return same_pair & t_hi & s_lo

    lmasks = [[level_mask(k, d) for k in log2s] for d in range(2)]

    lbs = []
    for d in range(2):
        p = lb_ref[d]
        e = jnp.exp(p - jnp.max(p, axis=0, keepdims=True))
        sm = e / jnp.sum(e, axis=0, keepdims=True)
        lb = jnp.zeros((1, W), F32)
        for r in range(1, layer + 1):
            lb = lb + sm[r:r + 1, :]
        lbs.append(lb)
    views = ((qf_ref, vf_ref, zf_ref, of_ref), (qb_ref, vb_ref, zb_ref, ob_ref))

    def boundary_rows(bs, c, d):
        off = c - 1 if d == 0 else c
        if 2 * c >= SUBLANES:
            parts = [jnp.broadcast_to(bs[p0 + off:p0 + off + 1, :], (2 * c, W)) for p0 in range(0, L, 2 * c)]
        else:
            parts = [jnp.where(row8 < 4,
                               jnp.broadcast_to(bs[g + off:g + off + 1, :], (SUBLANES, W)),
                               jnp.broadcast_to(bs[g + 4 + off:g + 5 + off, :], (SUBLANES, W)))
                     for g in range(0, L, SUBLANES)]
        return parts[0] if len(parts) == 1 else jnp.concatenate(parts, axis=0)

    for idx in range(nch):
        for d in range(2):
            q_ref, v_ref, z_ref, o_ref = views[d]
            lb = lbs[d]
            c0 = (idx if d == 0 else nch - 1 - idx) * L
            rows = slice(c0, c0 + L)
            z = z_ref[rows, :]
            f = lb + (1.0 - lb) * _sigmoid(z)
            kk = (1.0 - lb) * _sigmoid(-z)
            b = _cumsum_rows(tris[d], jnp.log(f))
            bs = b_scr.at[2 * idx + d]
            bs[...] = b
            bl = b[L - 1:L, :] if d == 0 else b[0:1, :]
            q_bf = q_ref[rows, :]
            q = q_bf.astype(F32)
            v_bf = v_ref[rows, :]
            kk_bf = kk.astype(BF16)
            qe = (q * jnp.exp(b)).astype(BF16)
            ke = (kk * jnp.exp(bl - b)).astype(BF16)
            el = jnp.exp(bl)

            qts, kss = [], []
            for k in log2s:
                c = 1 << k
                if c == 1:
                    qts.append((q * f).astype(BF16))
                    kss.append(kk_bf)
                else:
                    e = jnp.exp(-jnp.abs(b - boundary_rows(bs, c, d)))
                    qts.append((q * e).astype(BF16))
                    kss.append((kk * e).astype(BF16))

            for h in range(B_HEADS):
                chain = d * B_HEADS + h
                fs = slice(h * B_F, (h + 1) * B_F)
                vs = slice(h * B_V, (h + 1) * B_V)
                a = jnp.where(eye, _dot_nt(q_bf[:, fs], kk_bf[:, fs]), 0.0)
                for li in range(len(log2s)):
                    a = jnp.where(lmasks[d][li], _dot_nt(qts[li][:, fs], kss[li][:, fs]), a)
                st = st_ref[chain]
                o_ref[rows, vs] = _dot_nt(qe[:, fs], st.astype(BF16)) + _dot(a.astype(BF16), v_bf[:, vs])
                st_ref[chain] = st * el[:, fs] + lax.dot_general(
                    v_bf[:, vs], ke[:, fs], (((0,), (0,)), ((), ())), preferred_element_type=F32)


def _hgrn_mixer(p_main, zgates, b_lb, layer, batch, seq):
    m = batch * seq
    rows = min(MIX_ROWS, seq)
    nb = seq // rows
    fwd = lambda b, j: b * nb + j
    bwd = lambda b, j: b * nb + nb - 1 - j

    def specs(row, zcol):
        return [pl.BlockSpec((rows, 1024), lambda b, j: (row(b, j), 0)),
                pl.BlockSpec((rows, 1024), lambda b, j: (row(b, j), 1)),
                pl.BlockSpec((rows, 1024), lambda b, j: (row(b, j), zcol))]

    return pl.pallas_call(
        functools.partial(_hgrn_kernel, chunk=B_CHUNK, layer=layer),
        grid=(batch, nb),
        in_specs=specs(fwd, 0) + specs(bwd, 1) + [_const_spec(b_lb.shape)],
        out_specs=[pl.BlockSpec((rows, D_MODEL), lambda b, j: (fwd(b, j), 0)),
                   pl.BlockSpec((rows, D_MODEL), lambda b, j: (bwd(b, j), 0))],
        out_shape=[jax.ShapeDtypeStruct((m, D_MODEL), F32)] * 2,
        scratch_shapes=[pltpu.VMEM((2 * B_HEADS, B_V, B_F), F32),
                        pltpu.VMEM((2 * (rows // B_CHUNK), B_CHUNK, B_HEADS * B_F), F32)],
        compiler_params=_params(("parallel", "arbitrary")),
        name="hgrn_mixer",
    )(p_main, p_main, zgates, p_main, p_main, zgates, b_lb)


def _mix_out_kernel(hf_ref, hb_ref, gate_ref, xq_ref, kv_ref, wo_ref, gh_ref, gp_ref, x_ref, o_ref,
                    *, head_dim, silu_gate):
    h = hf_ref[...] + hb_ref[...]
    parts = []
    for c0 in range(0, D_MODEL, head_dim):
        hh = h[:, c0:c0 + head_dim]
        parts.append(hh * lax.rsqrt(jnp.mean(hh * hh, axis=-1, keepdims=True) + EPS))
    gate = gate_ref[...].astype(F32)
    gate = gate * _sigmoid(gate) if silu_gate else _sigmoid(gate)
    mix = (jnp.concatenate(parts, axis=-1) * gh_ref[...] * gate).astype(BF16)
    y = _dot(mix, wo_ref[0:D_MODEL, :])

    xa = []
    for hd in range(X_HEADS):
        cs = slice(hd * X_HD, (hd + 1) * X_HD)
        s = _dot_nt(xq_ref[:, cs], kv_ref[:, cs]) * (X_HD ** -0.5)
        e = jnp.exp(s - jnp.max(s, axis=-1, keepdims=True))
        pv = _dot(e.astype(BF16), kv_ref[:, X_W + hd * X_HD:X_W + (hd + 1) * X_HD])
        xa.append((pv / jnp.sum(e, axis=-1, keepdims=True)).astype(BF16))
    y = y + _dot(jnp.concatenate(xa, axis=-1), wo_ref[D_MODEL:D_MODEL + X_W, :])
    o_ref[...] = x_ref[...] + _rms(y, gp_ref[...])


def _mix_out(hf, hb, p_main, kv, w_out, g_head, g_post, x2d, batch, seq, head_dim, silu_gate):
    m = batch * seq
    tm = min(ROW_TILE, seq)
    nb = seq // tm
    row = lambda b, j: b * nb + j
    tile = pl.BlockSpec((tm, D_MODEL), lambda b, j: (row(b, j), 0))
    return pl.pallas_call(
        functools.partial(_mix_out_kernel, head_dim=head_dim, silu_gate=silu_gate),
        grid=(batch, nb),
        in_specs=[tile, tile,
                  pl.BlockSpec((tm, D_MODEL), lambda b, j: (row(b, j), GATE_COL_BLOCK)),
                  pl.BlockSpec((tm, X_W), lambda b, j: (row(b, j), XQ_COL_BLOCK)),
                  pl.BlockSpec((N_MEM, 2 * X_W), lambda b, j: (b, 0)),
                  _const_spec(w_out.shape), _const_spec((1, D_MODEL)), _const_spec((1, D_MODEL)),
                  tile],
        out_specs=tile,
        out_shape=jax.ShapeDtypeStruct((m, D_MODEL), F32),
        compiler_params=_params(("parallel", "parallel")),
        name="mix_out",
    )(hf, hb, p_main, p_main, kv, w_out, g_head.reshape(1, D_MODEL), g_post.reshape(1, D_MODEL), x2d)


HALO = 16


def _ffn_kernel(x_ref, xp_ref, xn_ref, gpre_ref, wup_ref, cw_ref, cb_ref, wdn_ref, gpost_ref, o_ref,
                hext_ref, ue_ref, act_ref, *, d_ff, fc, groups):
    tm = x_ref.shape[0]
    j = pl.program_id(1)
    gpre = gpre_ref[...]
    x = x_ref[...]
    has_prev = (j > 0).astype(F32)
    has_next = (j < pl.num_programs(1) - 1).astype(F32)
    hext_ref[0:HALO, :] = (_rms(xp_ref[...], gpre) * has_prev).astype(BF16)
    hext_ref[HALO:HALO + tm, :] = _rms(x, gpre).astype(BF16)
    hext_ref[HALO + tm:HALO + tm + HALO, :] = (_rms(xn_ref[...], gpre) * has_next).astype(BF16)

    n_chunks = d_ff // fc
    per_group = -(-n_chunks // groups)
    f = None
    for ci in range(n_chunks):
        c0 = ci * fc
        conv = []
        for part, base in enumerate((c0, d_ff + c0)):
            cs = slice(base, base + fc)
            ue = ue_ref.at[2 * (ci % 2) + part]
            ue[...] = _dot(hext_ref[...], wup_ref[:, cs])
            conv.append(ue[HALO - 1:HALO - 1 + tm, :] * cw_ref[0:1, cs] + ue[HALO:HALO + tm, :] * cw_ref[1:2, cs]
                        + ue[HALO + 1:HALO + 1 + tm, :] * cw_ref[2:3, cs] + cb_ref[:, cs])
        a, g = conv
        act_ref[:, c0:c0 + fc] = (g * _sigmoid(g) * a).astype(BF16)
        if (ci + 1) % per_group == 0 or ci == n_chunks - 1:
            k0 = (ci // per_group) * per_group * fc
            part_f = _dot(act_ref[:, k0:c0 + fc], wdn_ref[k0:c0 + fc, :])
            f = part_f if f is None else f + part_f
    o_ref[...] = x + _rms(f, gpost_ref[...])


def _ffn(x2d, g_pre, w_up, conv_w, conv_b, w_down, g_post, batch, seq):
    m = batch * seq
    tm = min(ROW_TILE, seq)
    nb = seq // tm
    hb = tm // HALO
    last_halo = m // HALO - 1
    d_ff = w_down.shape[0]
    tile = pl.BlockSpec((tm, D_MODEL), lambda b, j: (b * nb + j, 0))
    return pl.pallas_call(
        functools.partial(_ffn_kernel, d_ff=d_ff, fc=FF_CHUNK, groups=FF_GROUPS),
        grid=(batch, nb),
        in_specs=[tile,
                  pl.BlockSpec((HALO, D_MODEL), lambda b, j: (jnp.maximum((b * nb + j) * hb - 1, 0), 0)),
                  pl.BlockSpec((HALO, D_MODEL), lambda b, j: (jnp.minimum((b * nb + j + 1) * hb, last_halo), 0)),
                  _const_spec((1, D_MODEL)), _const_spec(w_up.shape), _const_spec(conv_w.shape),
                  _const_spec((1, 2 * d_ff)), _const_spec(w_down.shape), _const_spec((1, D_MODEL))],
        out_specs=tile,
        out_shape=jax.ShapeDtypeStruct((m, D_MODEL), F32),
        scratch_shapes=[pltpu.VMEM((tm + 2 * HALO, D_MODEL), BF16),
                        pltpu.VMEM((4, tm + 2 * HALO, FF_CHUNK), F32),
                        pltpu.VMEM((tm, d_ff), BF16)],
        compiler_params=_params(("parallel", "parallel")),
        name="ffn",
    )(x2d, x2d, x2d, g_pre.reshape(1, D_MODEL), w_up, conv_w, conv_b.reshape(1, 2 * d_ff), w_down,
      g_post.reshape(1, D_MODEL))


def _encoder_stack(x, mem, g_pre_mix, g_post_mix, g_mem, a_w_main, a_w_gate, a_bias, a_g_head, a_w_out,
                   b_w_main, b_w_gate, b_lb, b_g_head, b_w_out, x_w_kv, g_pre_ffn, g_post_ffn,
                   f_w_up, f_conv_w, f_conv_b, f_w_down):
    batch, seq, d = x.shape
    x2d = x.reshape(batch * seq, d)
    mem2d = mem.reshape(batch * N_MEM, d)
    depth = g_pre_mix.shape[0]
    for i in range(depth):
        j = i // 2
        kv = _norm_proj(mem2d, g_mem[i], x_w_kv[i])
        if i % 2 == 0:
            p_main, gates = _norm_proj(x2d, g_pre_mix[i], a_w_main[j], a_w_gate[j])
            hf, hb = _mlstm_mixer(p_main, gates, a_bias[j], batch, seq)
            x2d = _mix_out(hf, hb, p_main, kv, a_w_out[j], a_g_head[j], g_post_mix[i], x2d, batch, seq,
                           head_dim=A_V, silu_gate=False)
        else:
            p_main, zgates = _norm_proj(x2d, g_pre_mix[i], b_w_main[j], b_w_gate[j])
            hf, hb = _hgrn_mixer(p_main, zgates, b_lb, j, batch, seq)
            x2d = _mix_out(hf, hb, p_main, kv, b_w_out[j], b_g_head[j], g_post_mix[i], x2d, batch, seq,
                           head_dim=B_V, silu_gate=True)
        x2d = _ffn(x2d, g_pre_ffn[i], f_w_up[i], f_conv_w[i], f_conv_b[i], f_w_down[i], g_post_ffn[i],
                   batch, seq)
    return x2d.reshape(batch, seq, d)


def kernel(x_prompt, x_sample, mem_prompt, mem_sample, g_pre_mix, g_post_mix, g_mem, a_w_in, a_b_gate, a_g_head, a_w_out, b_w_in, b_lb, b_g_head, b_w_out, x_w_kv, g_pre_ffn, g_post_ffn, f_w_up, f_conv_w, f_conv_b, f_w_down):
    a_rec = 2 * A_HEADS * A_QK + 2 * A_HEADS * A_V
    n_gate = 4 * A_HEADS
    a_w_main = jnp.concatenate([a_w_in[:, :, :a_rec], a_w_in[:, :, a_rec + n_gate:]], axis=-1).astype(BF16)
    a_w_gate = jnp.pad(a_w_in[:, :, a_rec:a_rec + n_gate], ((0, 0), (0, 0), (0, A_GATE_LANES - n_gate))).astype(BF16)
    a_bias = jnp.pad(a_b_gate, ((0, 0), (0, A_GATE_LANES - n_gate))).reshape(-1, 1, A_GATE_LANES)
    w = B_HEADS * B_F
    b_w_main = jnp.concatenate([b_w_in[:, :, :2 * w], b_w_in[:, :, 4 * w:]], axis=-1).astype(BF16)
    b_w_gate = b_w_in[:, :, 2 * w:4 * w].astype(BF16)
    shared = (g_pre_mix, g_post_mix, g_mem, a_w_main, a_w_gate, a_bias, a_g_head, a_w_out.astype(BF16),
              b_w_main, b_w_gate, b_lb, b_g_head, b_w_out.astype(BF16), x_w_kv.astype(BF16),
              g_pre_ffn, g_post_ffn, f_w_up.astype(BF16), f_conv_w, f_conv_b, f_w_down.astype(BF16))
    return (_encoder_stack(x_prompt, mem_prompt, *shared), _encoder_stack(x_sample, mem_sample, *shared))
```

```python
import functools

import jax
import jax.numpy as jnp
from jax import lax
from jax.experimental import pallas as pl
from jax.experimental.pallas import tpu as pltpu

F32 = jnp.float32
BF16 = jnp.bfloat16
EPS = 1e-6

D_MODEL = 1024
N_MEM = 256
X_HEADS = 4
X_HD = 128
X_W = X_HEADS * X_HD
A_HEADS = 4
A_QK = 128
A_V = 256
B_HEADS = 8
B_F = 128
B_V = 128

GATE_COL_BLOCK = 2
XQ_COL_BLOCK = 6
LANES = 128
SUBLANES = 8
A_GATE_LANES = 2 * LANES
A_VA = A_V + LANES

VMEM_LIMIT_BYTES = 56 * 1024 * 1024

ROW_TILE = 512
MIX_ROWS = 512
A_CHUNK = 128
B_CHUNK = 128
FF_CHUNK = 256
FF_GROUPS = 2


def _params(sem):
    return pltpu.CompilerParams(dimension_semantics=sem, vmem_limit_bytes=VMEM_LIMIT_BYTES)


def _const_spec(shape):
    nd = len(shape)
    return pl.BlockSpec(shape, lambda *_: (0,) * nd, pipeline_mode=pl.Buffered(1))


def _rms(x, g):
    ms = jnp.mean(x * x, axis=-1, keepdims=True)
    return x * lax.rsqrt(ms + EPS) * g


def _sigmoid(x):
    return 1.0 / (1.0 + jnp.exp(-x))


def _log_sigmoid(x):
    return jnp.minimum(x, 0.0) - jnp.log(1.0 + jnp.exp(-jnp.abs(x)))


def _dot(a, b):
    return jnp.dot(a, b, preferred_element_type=F32)


def _dot_nt(a, b):
    return lax.dot_general(a, b, (((1,), (1,)), ((), ())), preferred_element_type=F32)


def _cumsum_rows(tri_bf, x):
    hi = x.astype(BF16)
    r1 = x - hi.astype(F32)
    mid = r1.astype(BF16)
    lo = (r1 - mid.astype(F32)).astype(BF16)
    return _dot(tri_bf, hi) + _dot(tri_bf, mid) + _dot(tri_bf, lo)


def _scan_rows(x, reverse, op):
    n = x.shape[0] // SUBLANES
    row8 = lax.broadcasted_iota(jnp.int32, (SUBLANES, 1), 0)
    ident = 0.0 if op is jnp.add else -jnp.inf
    parts = [x[i * SUBLANES:(i + 1) * SUBLANES] for i in range(n)]
    for s in (1, 2, 4):
        for i in range(n):
            p = parts[i]
            if not reverse:
                parts[i] = op(p, jnp.where(row8 >= s, pltpu.roll(p, s, 0), ident))
            else:
                parts[i] = op(p, jnp.where(row8 < SUBLANES - s, pltpu.roll(p, SUBLANES - s, 0), ident))
    carry = None
    for i in (range(n) if not reverse else range(n - 1, -1, -1)):
        if carry is not None:
            parts[i] = op(parts[i], carry)
        edge = parts[i][SUBLANES - 1:SUBLANES] if not reverse else parts[i][0:1]
        carry = jnp.broadcast_to(edge, (SUBLANES, x.shape[1]))
    return jnp.concatenate(parts, axis=0)


def _lane_bcast(x, c):
    return jnp.broadcast_to(x[:, c:c + 1], (x.shape[0], LANES))


def _norm_proj_kernel(x_ref, g_ref, wa_ref, *rest, col_chunk):
    if len(rest) == 3:
        wb_ref, oa_ref, ob_ref = rest
    else:
        (oa_ref,) = rest
        wb_ref = ob_ref = None
    h = _rms(x_ref[...], g_ref[...]).astype(BF16)
    for w_ref, o_ref in ((wa_ref, oa_ref), (wb_ref, ob_ref)):
        if w_ref is None:
            continue
        n = w_ref.shape[1]
        for c0 in range(0, n, col_chunk):
            cw = min(col_chunk, n - c0)
            o_ref[:, c0:c0 + cw] = _dot(h, w_ref[:, c0:c0 + cw]).astype(o_ref.dtype)


def _norm_proj(x2d, g, wa, wb=None):
    m, d = x2d.shape
    tm = min(ROW_TILE, m)
    in_specs = [pl.BlockSpec((tm, d), lambda i: (i, 0)), _const_spec((1, d)), _const_spec(wa.shape)]
    out_shape = [jax.ShapeDtypeStruct((m, wa.shape[1]), BF16)]
    out_specs = [pl.BlockSpec((tm, wa.shape[1]), lambda i: (i, 0))]
    args = [x2d, g.reshape(1, d), wa]
    if wb is not None:
        in_specs.append(_const_spec(wb.shape))
        out_shape.append(jax.ShapeDtypeStruct((m, wb.shape[1]), F32))
        out_specs.append(pl.BlockSpec((tm, wb.shape[1]), lambda i: (i, 0)))
        args.append(wb)
    out = pl.pallas_call(
        functools.partial(_norm_proj_kernel, col_chunk=512),
        grid=(m // tm,),
        in_specs=in_specs,
        out_specs=out_specs,
        out_shape=out_shape,
        compiler_params=_params(("parallel",)),
        name="norm_proj",
    )(*args)
    return out if wb is not None else out[0]


def _mlstm_kernel(qf_ref, kf_ref, vf_ref, gf_ref, qb_ref, kb_ref, vb_ref, gb_ref, bias_ref,
                  hf_ref, hb_ref, c_ref, m_ref, *, chunk):
    L = chunk
    nch = qf_ref.shape[0] // L
    scale = A_QK ** -0.5

    @pl.when(pl.program_id(1) == 0)
    def _():
        c_ref[...] = jnp.zeros_like(c_ref)
        m_ref[...] = jnp.zeros_like(m_ref)

    ti = lax.broadcasted_iota(jnp.int32, (L, L), 0)
    si = lax.broadcasted_iota(jnp.int32, (L, L), 1)
    masks = (si <= ti, si >= ti)
    tris = (masks[0].astype(BF16), masks[1].astype(BF16))
    bias_i = bias_ref[:, 0:LANES]
    bias_f = bias_ref[:, LANES:2 * LANES]
    ones = jnp.ones((L, LANES), BF16)
    views = ((qf_ref, kf_ref, vf_ref, gf_ref, hf_ref), (qb_ref, kb_ref, vb_ref, gb_ref, hb_ref))

    for idx in range(nch):
        for d in range(2):
            q_ref, k_ref, v_ref, g_ref, h_ref = views[d]
            c0 = (idx if d == 0 else nch - 1 - idx) * L
            rows = slice(c0, c0 + L)
            li = g_ref[rows, 0:LANES] + bias_i
            cum = _cumsum_rows(tris[d], _log_sigmoid(g_ref[rows, LANES:2 * LANES] + bias_f))
            u = li - cum
            cm = _scan_rows(u, d == 1, jnp.maximum)
            u_t = u.T
            for hd in range(A_HEADS):
                chain = d * A_HEADS + hd
                b_bc = _lane_bcast(cum, chain)
                cm_bc = _lane_bcast(cm, chain)
                u_row = u_t[chain:chain + 1, :]
                bl = b_bc[L - 1:L, :] if d == 0 else b_bc[0:1, :]
                q = q_ref[rows, hd * A_QK:(hd + 1) * A_QK]
                k = k_ref[rows, hd * A_QK:(hd + 1) * A_QK]
                v_aug = jnp.concatenate([v_ref[rows, hd * A_V:(hd + 1) * A_V], ones], axis=1)
                s = _dot_nt(q, k) * jnp.where(masks[d], jnp.exp(u_row - cm_bc), 0.0)
                g_row = u_row + bl
                g_max = jnp.max(g_row, axis=1, keepdims=True)
                kw_t = (k.T.astype(F32) * jnp.exp(g_row - g_max)).astype(BF16)
                dca = _dot(kw_t, v_aug)
                m_old = m_ref[chain]
                c_old = c_ref[chain]
                mx = jnp.maximum(m_old, cm_bc)
                a_loc = scale * jnp.exp(cm_bc - mx)
                a_int = scale * jnp.exp(m_old - mx)
                lhs = jnp.concatenate([(a_loc * s).astype(BF16), (a_int * q.astype(F32)).astype(BF16)], axis=1)
                nd = _dot(lhs, jnp.concatenate([v_aug, c_old.astype(BF16)], axis=0))
                rden = 1.0 / jnp.maximum(jnp.abs(nd[:, A_V:]), jnp.exp(-(b_bc + mx)))
                for c1 in range(0, A_V, LANES):
                    h_ref[rows, hd * A_V + c1:hd * A_V + c1 + LANES] = nd[:, c1:c1 + LANES] * rden
                g_max_b = jnp.broadcast_to(g_max, (1, LANES))
                m_new = jnp.maximum(bl + m_old, g_max_b)
                dec = jnp.exp(bl + m_old - m_new)
                inc = jnp.exp(g_max_b - m_new)
                n_tiles = A_VA // LANES
                c_ref[chain] = (jnp.concatenate([dec] * n_tiles, axis=1) * c_old
                                + jnp.concatenate([inc] * n_tiles, axis=1) * dca)
                m_ref[chain] = m_new


def _mlstm_mixer(p_main, gates, bias, batch, seq):
    m = batch * seq
    rows = min(MIX_ROWS, seq)
    nb = seq // rows
    fwd = lambda b, j: b * nb + j
    bwd = lambda b, j: b * nb + nb - 1 - j

    def specs(row):
        return [pl.BlockSpec((rows, 512), lambda b, j: (row(b, j), 0)),
                pl.BlockSpec((rows, 512), lambda b, j: (row(b, j), 1)),
                pl.BlockSpec((rows, 1024), lambda b, j: (row(b, j), 1)),
                pl.BlockSpec((rows, A_GATE_LANES), lambda b, j: (row(b, j), 0))]

    n_chain = 2 * A_HEADS
    return pl.pallas_call(
        functools.partial(_mlstm_kernel, chunk=A_CHUNK),
        grid=(batch, nb),
        in_specs=specs(fwd) + specs(bwd) + [_const_spec((1, A_GATE_LANES))],
        out_specs=[pl.BlockSpec((rows, D_MODEL), lambda b, j: (fwd(b, j), 0)),
                   pl.BlockSpec((rows, D_MODEL), lambda b, j: (bwd(b, j), 0))],
        out_shape=[jax.ShapeDtypeStruct((m, D_MODEL), F32)] * 2,
        scratch_shapes=[pltpu.VMEM((n_chain, A_QK, A_VA), F32),
                        pltpu.VMEM((n_chain, 1, LANES), F32)],
        compiler_params=_params(("parallel", "arbitrary")),
        name="mlstm_mixer",
    )(p_main, p_main, p_main, gates, p_main, p_main, p_main, gates, bias)


def _hgrn_kernel(qf_ref, vf_ref, zf_ref, qb_ref, vb_ref, zb_ref, lb_ref, of_ref, ob_ref, st_ref,
                 *, chunk, layer):
    L = chunk
    nch = qf_ref.shape[0] // L
    W = B_HEADS * B_F

    @pl.when(pl.program_id(1) == 0)
    def _():
        st_ref[...] = jnp.zeros_like(st_ref)

    ti = lax.broadcasted_iota(jnp.int32, (L, L), 0)
    si = lax.broadcasted_iota(jnp.int32, (L, L), 1)
    eye = ti == si
    row8 = lax.broadcasted_iota(jnp.int32, (SUBLANES, 1), 0)
    log2s = list(range(L.bit_length() - 2, -1, -1))

    def level_mask(k, d):
        same_pair = lax.shift_right_logical(ti, k + 1) == lax.shift_right_logical(si, k + 1)
        t_hi = (lax.shift_right_logical(ti, k) & 1) == (1 if d == 0 else 0)
        s_lo = (lax.shift_right_logical(si, k) & 1) == (0 if d == 0 else 1)
        return same_pair & t_hi & s_lo

    lmasks = [[level_mask(k, d) for k in log2s] for d in range(2)]

    lbs = []
    for d in range(2):
        p = lb_ref[d]
        e = jnp.exp(p - jnp.max(p, axis=0, keepdims=True))
        sm = e / jnp.sum(e, axis=0, keepdims=True)
        lb = jnp.zeros((1, W), F32)
        for r in range(1, layer + 1):
            lb = lb + sm[r:r + 1, :]
        lbs.append(lb)
    views = ((qf_ref, vf_ref, zf_ref, of_ref), (qb_ref, vb_ref, zb_ref, ob_ref))

    def boundary_rows(b, c, d):
        off = c - 1 if d == 0 else c
        if 2 * c >= SUBLANES:
            parts = [jnp.broadcast_to(b[p0 + off:p0 + off + 1, :], (2 * c, B_F)) for p0 in range(0, L, 2 * c)]
        else:
            parts = [jnp.where(row8 < 4,
                               jnp.broadcast_to(b[g + off:g + off + 1, :], (SUBLANES, B_F)),
                               jnp.broadcast_to(b[g + 4 + off:g + 5 + off, :], (SUBLANES, B_F)))
                     for g in range(0, L, SUBLANES)]
        return parts[0] if len(parts) == 1 else jnp.concatenate(parts, axis=0)

    for idx in range(nch):
        for d in range(2):
            q_ref, v_ref, z_ref, o_ref = views[d]
            c0 = (idx if d == 0 else nch - 1 - idx) * L
            rows = slice(c0, c0 + L)
            for h in range(B_HEADS):
                chain = d * B_HEADS + h
                fs = slice(h * B_F, (h + 1) * B_F)
                vs = slice(h * B_V, (h + 1) * B_V)
                lb = lbs[d][:, fs]
                th = jnp.tanh(0.5 * z_ref[rows, fs])
                f = lb + (1.0 - lb) * (0.5 + 0.5 * th)
                kk = (1.0 - lb) * (0.5 - 0.5 * th)
                b = _scan_rows(jnp.log(f), d == 1, jnp.add)
                bl = b[L - 1:L, :] if d == 0 else b[0:1, :]
                q_bf = q_ref[rows, fs]
                q = q_bf.astype(F32)
                v_bf = v_ref[rows, vs]
                kk_bf = kk.astype(BF16)
                a = jnp.where(eye, _dot_nt(q_bf, kk_bf), 0.0)
                for li, k in enumerate(log2s):
                    c = 1 << k
                    if c == 1:
                        qt, ks = (q * f).astype(BF16), kk_bf
                    else:
                        e = jnp.exp(-jnp.abs(b - boundary_rows(b, c, d)))
                        qt, ks = (q * e).astype(BF16), (kk * e).astype(BF16)
                    a = jnp.where(lmasks[d][li], _dot_nt(qt, ks), a)
                st = st_ref[chain]
                qe = (q * jnp.exp(b)).astype(BF16)
                ke = (kk * jnp.exp(bl - b)).astype(BF16)
                o_ref[rows, vs] = _dot_nt(qe, st.astype(BF16)) + _dot(a.astype(BF16), v_bf)
                st_ref[chain] = st * jnp.exp(bl) + lax.dot_general(
                    v_bf, ke, (((0,), (0,)), ((), ())), preferred_element_type=F32)


def _hgrn_mixer(p_main, zgates, b_lb, layer, batch, seq):
    m = batch * seq
    rows = min(MIX_ROWS, seq)
    nb = seq // rows
    fwd = lambda b, j: b * nb + j
    bwd = lambda b, j: b * nb + nb - 1 - j

    def specs(row, zcol):
        return [pl.BlockSpec((rows, 1024), lambda b, j: (row(b, j), 0)),
                pl.BlockSpec((rows, 1024), lambda b, j: (row(b, j), 1)),
                pl.BlockSpec((rows, 1024), lambda b, j: (row(b, j), zcol))]

    return pl.pallas_call(
        functools.partial(_hgrn_kernel, chunk=B_CHUNK, layer=layer),
        grid=(batch, nb),
        in_specs=specs(fwd, 0) + specs(bwd, 1) + [_const_spec(b_lb.shape)],
        out_specs=[pl.BlockSpec((rows, D_MODEL), lambda b, j: (fwd(b, j), 0)),
                   pl.BlockSpec((rows, D_MODEL), lambda b, j: (bwd(b, j), 0))],
        out_shape=[jax.ShapeDtypeStruct((m, D_MODEL), F32)] * 2,
        scratch_shapes=[pltpu.VMEM((2 * B_HEADS, B_V, B_F), F32)],
        compiler_params=_params(("parallel", "arbitrary")),
        name="hgrn_mixer",
    )(p_main, p_main, zgates, p_main, p_main, zgates, b_lb)


def _mix_out_kernel(hf_ref, hb_ref, gate_ref, xq_ref, kv_ref, wo_ref, gh_ref, gp_ref, x_ref, o_ref,
                    *, head_dim, silu_gate):
    h = hf_ref[...] + hb_ref[...]
    parts = []
    for c0 in range(0, D_MODEL, head_dim):
        hh = h[:, c0:c0 + head_dim]
        parts.append(hh * lax.rsqrt(jnp.mean(hh * hh, axis=-1, keepdims=True) + EPS))
    gate = gate_ref[...].astype(F32)
    gate = gate * _sigmoid(gate) if silu_gate else _sigmoid(gate)
    mix = (jnp.concatenate(parts, axis=-1) * gh_ref[...] * gate).astype(BF16)
    y = _dot(mix, wo_ref[0:D_MODEL, :])

    xa = []
    for hd in range(X_HEADS):
        cs = slice(hd * X_HD, (hd + 1) * X_HD)
        s = _dot_nt(xq_ref[:, cs], kv_ref[:, cs]) * (X_HD ** -0.5)
        e = jnp.exp(s - jnp.max(s, axis=-1, keepdims=True))
        pv = _dot(e.astype(BF16), kv_ref[:, X_W + hd * X_HD:X_W + (hd + 1) * X_HD])
        xa.append((pv / jnp.sum(e, axis=-1, keepdims=True)).astype(BF16))
    y = y + _dot(jnp.concatenate(xa, axis=-1), wo_ref[D_MODEL:D_MODEL + X_W, :])
    o_ref[...] = x_ref[...] + _rms(y, gp_ref[...])


def _mix_out(hf, hb, p_main, kv, w_out, g_head, g_post, x2d, batch, seq, head_dim, silu_gate):
    m = batch * seq
    tm = min(ROW_TILE, seq)
    nb = seq // tm
    row = lambda b, j: b * nb + j
    tile = pl.BlockSpec((tm, D_MODEL), lambda b, j: (row(b, j), 0))
    return pl.pallas_call(
        functools.partial(_mix_out_kernel, head_dim=head_dim, silu_gate=silu_gate),
        grid=(batch, nb),
        in_specs=[tile, tile,
                  pl.BlockSpec((tm, D_MODEL), lambda b, j: (row(b, j), GATE_COL_BLOCK)),
                  pl.BlockSpec((tm, X_W), lambda b, j: (row(b, j), XQ_COL_BLOCK)),
                  pl.BlockSpec((N_MEM, 2 * X_W), lambda b, j: (b, 0)),
                  _const_spec(w_out.shape), _const_spec((1, D_MODEL)), _const_spec((1, D_MODEL)),
                  tile],
        out_specs=tile,
        out_shape=jax.ShapeDtypeStruct((m, D_MODEL), F32),
        compiler_params=_params(("parallel", "parallel")),
        name="mix_out",
    )(hf, hb, p_main, p_main, kv, w_out, g_head.reshape(1, D_MODEL), g_post.reshape(1, D_MODEL), x2d)


HALO = 16


def _ffn_kernel(x_ref, xp_ref, xn_ref, gpre_ref, wup_ref, cw_ref, cb_ref, wdn_ref, gpost_ref, o_ref,
                hext_ref, ue_ref, act_ref, *, d_ff, fc, groups):
    tm = x_ref.shape[0]
    j = pl.program_id(1)
    gpre = gpre_ref[...]
    x = x_ref[...]
    has_prev = (j > 0).astype(F32)
    has_next = (j < pl.num_programs(1) - 1).astype(F32)
    hext_ref[0:HALO, :] = (_rms(xp_ref[...], gpre) * has_prev).astype(BF16)
    hext_ref[HALO:HALO + tm, :] = _rms(x, gpre).astype(BF16)
    hext_ref[HALO + tm:HALO + tm + HALO, :] = (_rms(xn_ref[...], gpre) * has_next).astype(BF16)

    n_chunks = d_ff // fc
    per_group = -(-n_chunks // groups)
    f = None
    for ci in range(n_chunks):
        c0 = ci * fc
        conv = []
        for part, base in enumerate((c0, d_ff + c0)):
            cs = slice(base, base + fc)
            ue = ue_ref.at[2 * (ci % 2) + part]
            ue[...] = _dot(hext_ref[...], wup_ref[:, cs])
            conv.append(ue[HALO - 1:HALO - 1 + tm, :] * cw_ref[0:1, cs] + ue[HALO:HALO + tm, :] * cw_ref[1:2, cs]
                        + ue[HALO + 1:HALO + 1 + tm, :] * cw_ref[2:3, cs] + cb_ref[:, cs])
        a, g = conv
        act_ref[:, c0:c0 + fc] = (g * _sigmoid(g) * a).astype(BF16)
        if (ci + 1) % per_group == 0 or ci == n_chunks - 1:
            k0 = (ci // per_group) * per_group * fc
            part_f = _dot(act_ref[:, k0:c0 + fc], wdn_ref[k0:c0 + fc, :])
            f = part_f if f is None else f + part_f
    o_ref[...] = x + _rms(f, gpost_ref[...])


def _ffn(x2d, g_pre, w_up, conv_w, conv_b, w_down, g_post, batch, seq):
    m = batch * seq
    tm = min(ROW_TILE, seq)
    nb = seq // tm
    hb = tm // HALO
    last_halo = m // HALO - 1
    d_ff = w_down.shape[0]
    tile = pl.BlockSpec((tm, D_MODEL), lambda b, j: (b * nb + j, 0))
    return pl.pallas_call(
        functools.partial(_ffn_kernel, d_ff=d_ff, fc=FF_CHUNK, groups=FF_GROUPS),
        grid=(batch, nb),
        in_specs=[tile,
                  pl.BlockSpec((HALO, D_MODEL), lambda b, j: (jnp.maximum((b * nb + j) * hb - 1, 0), 0)),
                  pl.BlockSpec((HALO, D_MODEL), lambda b, j: (jnp.minimum((b * nb + j + 1) * hb, last_halo), 0)),
                  _const_spec((1, D_MODEL)), _const_spec(w_up.shape), _const_spec(conv_w.shape),
                  _const_spec((1, 2 * d_ff)), _const_spec(w_down.shape), _const_spec((1, D_MODEL))],
        out_specs=tile,
        out_shape=jax.ShapeDtypeStruct((m, D_MODEL), F32),
        scratch_shapes=[pltpu.VMEM((tm + 2 * HALO, D_MODEL), BF16),
                        pltpu.VMEM((4, tm + 2 * HALO, FF_CHUNK), F32),
                        pltpu.VMEM((tm, d_ff), BF16)],
        compiler_params=_params(("parallel", "parallel")),
        name="ffn",
    )(x2d, x2d, x2d, g_pre.reshape(1, D_MODEL), w_up, conv_w, conv_b.reshape(1, 2 * d_ff), w_down,
      g_post.reshape(1, D_MODEL))


def _encoder_stack(x, mem, g_pre_mix, g_post_mix, g_mem, a_w_main, a_w_gate, a_bias, a_g_head, a_w_out,
                   b_w_main, b_w_gate, b_lb, b_g_head, b_w_out, x_w_kv, g_pre_ffn, g_post_ffn,
                   f_w_up, f_conv_w, f_conv_b, f_w_down):
    batch, seq, d = x.shape
    x2d = x.reshape(batch * seq, d)
    mem2d = mem.reshape(batch * N_MEM, d)
    depth = g_pre_mix.shape[0]
    for i in range(depth):
        j = i // 2
        kv = _norm_proj(mem2d, g_mem[i], x_w_kv[i])
        if i % 2 == 0:
            p_main, gates = _norm_proj(x2d, g_pre_mix[i], a_w_main[j], a_w_gate[j])
            hf, hb = _mlstm_mixer(p_main, gates, a_bias[j], batch, seq)
            x2d = _mix_out(hf, hb, p_main, kv, a_w_out[j], a_g_head[j], g_post_mix[i], x2d, batch, seq,
                           head_dim=A_V, silu_gate=False)
        else:
            p_main, zgates = _norm_proj(x2d, g_pre_mix[i], b_w_main[j], b_w_gate[j])
            hf, hb = _hgrn_mixer(p_main, zgates, b_lb, j, batch, seq)
            x2d = _mix_out(hf, hb, p_main, kv, b_w_out[j], b_g_head[j], g_post_mix[i], x2d, batch, seq,
                           head_dim=B_V, silu_gate=True)
        x2d = _ffn(x2d, g_pre_ffn[i], f_w_up[i], f_conv_w[i], f_conv_b[i], f_w_down[i], g_post_ffn[i],
                   batch, seq)
    return x2d.reshape(batch, seq, d)


def kernel(x_prompt, x_sample, mem_prompt, mem_sample, g_pre_mix, g_post_mix, g_mem, a_w_in, a_b_gate, a_g_head, a_w_out, b_w_in, b_lb, b_g_head, b_w_out, x_w_kv, g_pre_ffn, g_post_ffn, f_w_up, f_conv_w, f_conv_b, f_w_down):
    a_rec = 2 * A_HEADS * A_QK + 2 * A_HEADS * A_V
    n_gate = 4 * A_HEADS
    a_w_main = jnp.concatenate([a_w_in[:, :, :a_rec], a_w_in[:, :, a_rec + n_gate:]], axis=-1).astype(BF16)
    perm_i = jnp.array([0, 1, 2, 3, 8, 9, 10, 11]) + a_rec
    perm_f = perm_i + A_HEADS

    def two_tiles(x):
        pad = [(0, 0)] * (x.ndim - 1) + [(0, LANES - 2 * A_HEADS)]
        return jnp.concatenate([jnp.pad(jnp.take(x, perm_i, axis=-1), pad),
                                jnp.pad(jnp.take(x, perm_f, axis=-1), pad)], axis=-1)

    a_w_gate = two_tiles(a_w_in).astype(BF16)
    a_bias = two_tiles(jnp.pad(a_b_gate, ((0, 0), (a_rec, 0)))).reshape(-1, 1, A_GATE_LANES)
    w = B_HEADS * B_F
    b_w_main = jnp.concatenate([b_w_in[:, :, :2 * w], b_w_in[:, :, 4 * w:]], axis=-1).astype(BF16)
    b_w_gate = b_w_in[:, :, 2 * w:4 * w].astype(BF16)
    shared = (g_pre_mix, g_post_mix, g_mem, a_w_main, a_w_gate, a_bias, a_g_head, a_w_out.astype(BF16),
              b_w_main, b_w_gate, b_lb, b_g_head, b_w_out.astype(BF16), x_w_kv.astype(BF16),
              g_pre_ffn, g_post_ffn, f_w_up.astype(BF16), f_conv_w, f_conv_b, f_w_down.astype(BF16))
    return (_encoder_stack(x_prompt, mem_prompt, *shared), _encoder_stack(x_sample, mem_sample, *shared))
```

```python
import functools

import jax
import jax.numpy as jnp
from jax import lax
from jax.experimental import pallas as pl
from jax.experimental.pallas import tpu as pltpu

F32 = jnp.float32
BF16 = jnp.bfloat16
EPS = 1e-6

D_MODEL = 1024
N_MEM = 256
X_HEADS = 4
X_HD = 128
X_W = X_HEADS * X_HD
A_HEADS = 4
A_QK = 128
A_V = 256
B_HEADS = 8
B_F = 128
B_V = 128

GATE_COL_BLOCK = 2
XQ_COL_BLOCK = 6
LANES = 128
SUBLANES = 8
A_GATE_LANES = 2 * LANES
A_VA = A_V + LANES

VMEM_LIMIT_BYTES = 56 * 1024 * 1024

ROW_TILE = 512
MIX_ROWS = 512
A_CHUNK = 128
B_CHUNK = 128
FF_CHUNK = 256
FF_GROUPS = 2
T_SLOTS = 8
LOG2E = 1.4426950408889634


def _params(sem):
    return pltpu.CompilerParams(dimension_semantics=sem, vmem_limit_bytes=VMEM_LIMIT_BYTES)


def _const_spec(shape):
    nd = len(shape)
    return pl.BlockSpec(shape, lambda *_: (0,) * nd, pipeline_mode=pl.Buffered(1))


def _rms(x, g):
    ms = jnp.mean(x * x, axis=-1, keepdims=True)
    return x * lax.rsqrt(ms + EPS) * g


def _sigmoid(x):
    return 1.0 / (1.0 + jnp.exp(-x))


def _log_sigmoid(x):
    return jnp.minimum(x, 0.0) - jnp.log(1.0 + jnp.exp(-jnp.abs(x)))


def _dot(a, b):
    return jnp.dot(a, b, preferred_element_type=F32)


def _dot_nt(a, b):
    return lax.dot_general(a, b, (((1,), (1,)), ((), ())), preferred_element_type=F32)


def _cumsum_rows(tri_bf, x):
    hi = x.astype(BF16)
    r1 = x - hi.astype(F32)
    mid = r1.astype(BF16)
    lo = (r1 - mid.astype(F32)).astype(BF16)
    return _dot(tri_bf, hi) + _dot(tri_bf, mid) + _dot(tri_bf, lo)


def _scan_rows(x, reverse, op):
    n = x.shape[0] // SUBLANES
    row8 = lax.broadcasted_iota(jnp.int32, (SUBLANES, 1), 0)
    ident = 0.0 if op is jnp.add else -jnp.inf
    parts = [x[i * SUBLANES:(i + 1) * SUBLANES] for i in range(n)]
    for s in (1, 2, 4):
        for i in range(n):
            p = parts[i]
            if not reverse:
                parts[i] = op(p, jnp.where(row8 >= s, pltpu.roll(p, s, 0), ident))
            else:
                parts[i] = op(p, jnp.where(row8 < SUBLANES - s, pltpu.roll(p, SUBLANES - s, 0), ident))
    carry = None
    for i in (range(n) if not reverse else range(n - 1, -1, -1)):
        if carry is not None:
            parts[i] = op(parts[i], carry)
        edge = parts[i][SUBLANES - 1:SUBLANES] if not reverse else parts[i][0:1]
        carry = jnp.broadcast_to(edge, (SUBLANES, x.shape[1]))
    return jnp.concatenate(parts, axis=0)


def _lane_bcast(x, c):
    return jnp.broadcast_to(x[:, c:c + 1], (x.shape[0], LANES))


def _norm_proj_kernel(x_ref, g_ref, wa_ref, *rest, col_chunk):
    if len(rest) == 3:
        wb_ref, oa_ref, ob_ref = rest
    else:
        (oa_ref,) = rest
        wb_ref = ob_ref = None
    h = _rms(x_ref[...], g_ref[...]).astype(BF16)
    for w_ref, o_ref in ((wa_ref, oa_ref), (wb_ref, ob_ref)):
        if w_ref is None:
            continue
        n = w_ref.shape[1]
        for c0 in range(0, n, col_chunk):
            cw = min(col_chunk, n - c0)
            o_ref[:, c0:c0 + cw] = _dot(h, w_ref[:, c0:c0 + cw]).astype(o_ref.dtype)


def _norm_proj(x2d, g, wa, wb=None):
    m, d = x2d.shape
    tm = min(ROW_TILE, m)
    in_specs = [pl.BlockSpec((tm, d), lambda i: (i, 0)), _const_spec((1, d)), _const_spec(wa.shape)]
    out_shape = [jax.ShapeDtypeStruct((m, wa.shape[1]), BF16)]
    out_specs = [pl.BlockSpec((tm, wa.shape[1]), lambda i: (i, 0))]
    args = [x2d, g.reshape(1, d), wa]
    if wb is not None:
        in_specs.append(_const_spec(wb.shape))
        out_shape.append(jax.ShapeDtypeStruct((m, wb.shape[1]), F32))
        out_specs.append(pl.BlockSpec((tm, wb.shape[1]), lambda i: (i, 0)))
        args.append(wb)
    out = pl.pallas_call(
        functools.partial(_norm_proj_kernel, col_chunk=512),
        grid=(m // tm,),
        in_specs=in_specs,
        out_specs=out_specs,
        out_shape=out_shape,
        compiler_params=_params(("parallel",)),
        name="norm_proj",
    )(*args)
    return out if wb is not None else out[0]


def _mlstm_kernel(qf_ref, kf_ref, vf_ref, gf_ref, qb_ref, kb_ref, vb_ref, gb_ref, bias_ref,
                  hf_ref, hb_ref, c_ref, m_ref, *, chunk):
    L = chunk
    nch = qf_ref.shape[0] // L
    scale = A_QK ** -0.5

    @pl.when(pl.program_id(1) == 0)
    def _():
        c_ref[...] = jnp.zeros_like(c_ref)
        m_ref[...] = jnp.zeros_like(m_ref)

    ti = lax.broadcasted_iota(jnp.int32, (L, L), 0)
    si = lax.broadcasted_iota(jnp.int32, (L, L), 1)
    masks = (si <= ti, si >= ti)
    tris = (masks[0].astype(BF16), masks[1].astype(BF16))
    bias_i = bias_ref[:, 0:LANES]
    bias_f = bias_ref[:, LANES:2 * LANES]
    ones = jnp.ones((L, LANES), BF16)
    views = ((qf_ref, kf_ref, vf_ref, gf_ref, hf_ref), (qb_ref, kb_ref, vb_ref, gb_ref, hb_ref))

    for idx in range(nch):
        for d in range(2):
            q_ref, k_ref, v_ref, g_ref, h_ref = views[d]
            c0 = (idx if d == 0 else nch - 1 - idx) * L
            rows = slice(c0, c0 + L)
            li = g_ref[rows, 0:LANES] + bias_i
            cum = _cumsum_rows(tris[d], _log_sigmoid(g_ref[rows, LANES:2 * LANES] + bias_f))
            u = li - cum
            cm = _scan_rows(u, d == 1, jnp.maximum)
            u_t = u.T
            for hd in range(A_HEADS):
                chain = d * A_HEADS + hd
                b_bc = _lane_bcast(cum, chain)
                cm_bc = _lane_bcast(cm, chain)
                u_row = u_t[chain:chain + 1, :]
                bl = b_bc[L - 1:L, :] if d == 0 else b_bc[0:1, :]
                q = q_ref[rows, hd * A_QK:(hd + 1) * A_QK]
                k = k_ref[rows, hd * A_QK:(hd + 1) * A_QK]
                v_aug = jnp.concatenate([v_ref[rows, hd * A_V:(hd + 1) * A_V], ones], axis=1)
                s = _dot_nt(q, k) * jnp.where(masks[d], jnp.exp(u_row - cm_bc), 0.0)
                g_row = u_row + bl
                g_max = jnp.max(g_row, axis=1, keepdims=True)
                kw_t = (k.T.astype(F32) * jnp.exp(g_row - g_max)).astype(BF16)
                dca = _dot(kw_t, v_aug)
                m_old = m_ref[chain]
                c_old = c_ref[chain]
                mx = jnp.maximum(m_old, cm_bc)
                a_loc = scale * jnp.exp(cm_bc - mx)
                a_int = scale * jnp.exp(m_old - mx)
                lhs = jnp.concatenate([(a_loc * s).astype(BF16), (a_int * q.astype(F32)).astype(BF16)], axis=1)
                nd = _dot(lhs, jnp.concatenate([v_aug, c_old.astype(BF16)], axis=0))
                rden = 1.0 / jnp.maximum(jnp.abs(nd[:, A_V:]), jnp.exp(-(b_bc + mx)))
                for c1 in range(0, A_V, LANES):
                    h_ref[rows, hd * A_V + c1:hd * A_V + c1 + LANES] = (nd[:, c1:c1 + LANES] * rden).astype(h_ref.dtype)
                g_max_b = jnp.broadcast_to(g_max, (1, LANES))
                m_new = jnp.maximum(bl + m_old, g_max_b)
                dec = jnp.exp(bl + m_old - m_new)
                inc = jnp.exp(g_max_b - m_new)
                n_tiles = A_VA // LANES
                c_ref[chain] = (jnp.concatenate([dec] * n_tiles, axis=1) * c_old
                                + jnp.concatenate([inc] * n_tiles, axis=1) * dca)
                m_ref[chain] = m_new


def _mlstm_mixer(p_main, gates, bias, batch, seq):
    m = batch * seq
    rows = min(MIX_ROWS, seq)
    nb = seq // rows
    fwd = lambda b, j: b * nb + j
    bwd = lambda b, j: b * nb + nb - 1 - j

    def specs(row):
        return [pl.BlockSpec((rows, 512), lambda b, j: (row(b, j), 0)),
                pl.BlockSpec((rows, 512), lambda b, j: (row(b, j), 1)),
                pl.BlockSpec((rows, 1024), lambda b, j: (row(b, j), 1)),
                pl.BlockSpec((rows, A_GATE_LANES), lambda b, j: (row(b, j), 0))]

    n_chain = 2 * A_HEADS
    return pl.pallas_call(
        functools.partial(_mlstm_kernel, chunk=A_CHUNK),
        grid=(batch, nb),
        in_specs=specs(fwd) + specs(bwd) + [_const_spec((1, A_GATE_LANES))],
        out_specs=[pl.BlockSpec((rows, D_MODEL), lambda b, j: (fwd(b, j), 0)),
                   pl.BlockSpec((rows, D_MODEL), lambda b, j: (bwd(b, j), 0))],
        out_shape=[jax.ShapeDtypeStruct((m, D_MODEL), BF16)] * 2,
        scratch_shapes=[pltpu.VMEM((n_chain, A_QK, A_VA), F32),
                        pltpu.VMEM((n_chain, 1, LANES), F32)],
        compiler_params=_params(("parallel", "arbitrary")),
        name="mlstm_mixer",
    )(p_main, p_main, p_main, gates, p_main, p_main, p_main, gates, bias)


def _hgrn_kernel(qf_ref, vf_ref, zf_ref, qb_ref, vb_ref, zb_ref, lb_ref, of_ref, ob_ref, st_ref, t_scr,
                 *, chunk, layer):
    L = chunk
    nch = qf_ref.shape[0] // L
    W = B_HEADS * B_F

    @pl.when(pl.program_id(1) == 0)
    def _():
        st_ref[...] = jnp.zeros_like(st_ref)

    n_slots = t_scr.shape[0]
    uses = [0]

    def dot_nt(a, b):
        slot = t_scr.at[uses[0] % n_slots]
        uses[0] += 1
        slot[...] = b.T
        return _dot(a, slot[...])

    ti = lax.broadcasted_iota(jnp.int32, (L, L), 0)
    si = lax.broadcasted_iota(jnp.int32, (L, L), 1)
    eye = ti == si
    row8 = lax.broadcasted_iota(jnp.int32, (SUBLANES, 1), 0)
    log2s = list(range(L.bit_length() - 2, -1, -1))

    def level_mask(k, d):
        same_pair = lax.shift_right_logical(ti, k + 1) == lax.shift_right_logical(si, k + 1)
        t_hi = (lax.shift_right_logical(ti, k) & 1) == (1 if d == 0 else 0)
        s_lo = (lax.shift_right_logical(si, k) & 1) == (0 if d == 0 else 1)
        return same_pair & t_hi & s_lo

    lmasks = [[level_mask(k, d) for k in log2s] for d in range(2)]

    lbs = []
    for d in range(2):
        p = lb_ref[d]
        e = jnp.exp(p - jnp.max(p, axis=0, keepdims=True))
        sm = e / jnp.sum(e, axis=0, keepdims=True)
        lb = jnp.zeros((1, W), F32)
        for r in range(1, layer + 1):
            lb = lb + sm[r:r + 1, :]
        lbs.append(lb)
    views = ((qf_ref, vf_ref, zf_ref, of_ref), (qb_ref, vb_ref, zb_ref, ob_ref))

    def boundary_rows(b, c, d):
        off = c - 1 if d == 0 else c
        if 2 * c >= SUBLANES:
            parts = [jnp.broadcast_to(b[p0 + off:p0 + off + 1, :], (2 * c, B_F)) for p0 in range(0, L, 2 * c)]
        else:
            parts = [jnp.where(row8 < 4,
                               jnp.broadcast_to(b[g + off:g + off + 1, :], (SUBLANES, B_F)),
                               jnp.broadcast_to(b[g + 4 + off:g + 5 + off, :], (SUBLANES, B_F)))
                     for g in range(0, L, SUBLANES)]
        return parts[0] if len(parts) == 1 else jnp.concatenate(parts, axis=0)

    for idx in range(nch):
        for d in range(2):
            q_ref, v_ref, z_ref, o_ref = views[d]
            c0 = (idx if d == 0 else nch - 1 - idx) * L
            rows = slice(c0, c0 + L)
            for h in range(B_HEADS):
                chain = d * B_HEADS + h
                fs = slice(h * B_F, (h + 1) * B_F)
                vs = slice(h * B_V, (h + 1) * B_V)
                lb = lbs[d][:, fs]
                th = jnp.tanh(0.5 * z_ref[rows, fs])
                f = lb + (1.0 - lb) * (0.5 + 0.5 * th)
                kk = (1.0 - lb) * (0.5 - 0.5 * th)
                b = _scan_rows(jnp.log(f), d == 1, jnp.add)
                bl = b[L - 1:L, :] if d == 0 else b[0:1, :]
                b2 = b * LOG2E
                q_bf = q_ref[rows, fs]
                q = q_bf.astype(F32)
                v_bf = v_ref[rows, vs]
                a = jnp.where(eye, dot_nt(q_bf, kk.astype(BF16)), 0.0)
                for li, k in enumerate(log2s):
                    c = 1 << k
                    t_upper = d == 0
                    if c >= SUBLANES:
                        lo_src, hi_src = (kk, q) if t_upper else (q, kk)
                        parts = []
                        for p0 in range(0, L, 2 * c):
                            parts += [lo_src[p0:p0 + c], hi_src[p0 + c:p0 + 2 * c]]
                        x = jnp.concatenate(parts, axis=0)
                    else:
                        is_upper = (lax.shift_right_logical(row8, k) & 1) == 1
                        is_t = is_upper if t_upper else jnp.logical_not(is_upper)
                        qq = q * f if c == 1 else q
                        x = jnp.concatenate([jnp.where(is_t, qq[g:g + SUBLANES], kk[g:g + SUBLANES])
                                             for g in range(0, L, SUBLANES)], axis=0)
                    if c > 1:
                        x = x * jnp.exp2(-jnp.abs(b2 - boundary_rows(b2, c, d)))
                    w = x.astype(BF16)
                    a = jnp.where(lmasks[d][li], dot_nt(w, w), a)
                st = st_ref[chain]
                qe = (q * jnp.exp2(b2)).astype(BF16)
                ke = (kk * jnp.exp(bl - b)).astype(BF16)
                o_ref[rows, vs] = (dot_nt(qe, st.astype(BF16)) + _dot(a.astype(BF16), v_bf)).astype(o_ref.dtype)
                st_ref[chain] = st * jnp.exp(bl) + lax.dot_general(
                    v_bf, ke, (((0,), (0,)), ((), ())), preferred_element_type=F32)


def _hgrn_mixer(p_main, zgates, b_lb, layer, batch, seq):
    m = batch * seq
    rows = min(MIX_ROWS, seq)
    nb = seq // rows
    fwd = lambda b, j: b * nb + j
    bwd = lambda b, j: b * nb + nb - 1 - j

    def specs(row, zcol):
        return [pl.BlockSpec((rows, 1024), lambda b, j: (row(b, j), 0)),
                pl.BlockSpec((rows, 1024), lambda b, j: (row(b, j), 1)),
                pl.BlockSpec((rows, 1024), lambda b, j: (row(b, j), zcol))]

    return pl.pallas_call(
        functools.partial(_hgrn_kernel, chunk=B_CHUNK, layer=layer),
        grid=(batch, nb),
        in_specs=specs(fwd, 0) + specs(bwd, 1) + [_const_spec(b_lb.shape)],
        out_specs=[pl.BlockSpec((rows, D_MODEL), lambda b, j: (fwd(b, j), 0)),
                   pl.BlockSpec((rows, D_MODEL), lambda b, j: (bwd(b, j), 0))],
        out_shape=[jax.ShapeDtypeStruct((m, D_MODEL), BF16)] * 2,
        scratch_shapes=[pltpu.VMEM((2 * B_HEADS, B_V, B_F), F32),
                        pltpu.VMEM((T_SLOTS, B_F, B_CHUNK), BF16)],
        compiler_params=_params(("parallel", "arbitrary")),
        name="hgrn_mixer",
    )(p_main, p_main, zgates, p_main, p_main, zgates, b_lb)


def _mix_out_kernel(hf_ref, hb_ref, gate_ref, xq_ref, kv_ref, wo_ref, gh_ref, gp_ref, x_ref, o_ref,
                    *, head_dim, silu_gate):
    h = hf_ref[...].astype(F32) + hb_ref[...].astype(F32)
    parts = []
    for c0 in range(0, D_MODEL, head_dim):
        hh = h[:, c0:c0 + head_dim]
        parts.append(hh * lax.rsqrt(jnp.mean(hh * hh, axis=-1, keepdims=True) + EPS))
    gate = gate_ref[...].astype(F32)
    gate = gate * _sigmoid(gate) if silu_gate else _sigmoid(gate)
    mix = (jnp.concatenate(parts, axis=-1) * gh_ref[...] * gate).astype(BF16)
    y = _dot(mix, wo_ref[0:D_MODEL, :])

    xa = []
    for hd in range(X_HEADS):
        cs = slice(hd * X_HD, (hd + 1) * X_HD)
        s = _dot_nt(xq_ref[:, cs], kv_ref[:, cs]) * (X_HD ** -0.5)
        e = jnp.exp(s - jnp.max(s, axis=-1, keepdims=True))
        pv = _dot(e.astype(BF16), kv_ref[:, X_W + hd * X_HD:X_W + (hd + 1) * X_HD])
        xa.append((pv / jnp.sum(e, axis=-1, keepdims=True)).astype(BF16))
    y = y + _dot(jnp.concatenate(xa, axis=-1), wo_ref[D_MODEL:D_MODEL + X_W, :])
    o_ref[...] = x_ref[...] + _rms(y, gp_ref[...])


def _mix_out(hf, hb, p_main, kv, w_out, g_head, g_post, x2d, batch, seq, head_dim, silu_gate):
    m = batch * seq
    tm = min(ROW_TILE, seq)
    nb = seq // tm
    row = lambda b, j: b * nb + j
    tile = pl.BlockSpec((tm, D_MODEL), lambda b, j: (row(b, j), 0))
    return pl.pallas_call(
        functools.partial(_mix_out_kernel, head_dim=head_dim, silu_gate=silu_gate),
        grid=(batch, nb),
        in_specs=[tile, tile,
                  pl.BlockSpec((tm, D_MODEL), lambda b, j: (row(b, j), GATE_COL_BLOCK)),
                  pl.BlockSpec((tm, X_W), lambda b, j: (row(b, j), XQ_COL_BLOCK)),
                  pl.BlockSpec((N_MEM, 2 * X_W), lambda b, j: (b, 0)),
                  _const_spec(w_out.shape), _const_spec((1, D_MODEL)), _const_spec((1, D_MODEL)),
                  tile],
        out_specs=tile,
        out_shape=jax.ShapeDtypeStruct((m, D_MODEL), F32),
        compiler_params=_params(("parallel", "parallel")),
        name="mix_out",
    )(hf, hb, p_main, p_main, kv, w_out, g_head.reshape(1, D_MODEL), g_post.reshape(1, D_MODEL), x2d)


HALO = 16


def _ffn_kernel(x_ref, xp_ref, xn_ref, gpre_ref, wup_ref, cw_ref, cb_ref, wdn_ref, gpost_ref, o_ref,
                hext_ref, ue_ref, act_ref, *, d_ff, fc, groups):
    tm = x_ref.shape[0]
    j = pl.program_id(1)
    gpre = gpre_ref[...]
    x = x_ref[...]
    has_prev = (j > 0).astype(F32)
    has_next = (j < pl.num_programs(1) - 1).astype(F32)
    hext_ref[0:HALO, :] = (_rms(xp_ref[...], gpre) * has_prev).astype(BF16)
    hext_ref[HALO:HALO + tm, :] = _rms(x, gpre).astype(BF16)
    hext_ref[HALO + tm:HALO + tm + HALO, :] = (_rms(xn_ref[...], gpre) * has_next).astype(BF16)

    n_chunks = d_ff // fc
    per_group = -(-n_chunks // groups)
    f = None
    for ci in range(n_chunks):
        c0 = ci * fc
        conv = []
        for part, base in enumerate((c0, d_ff + c0)):
            cs = slice(base, base + fc)
            ue = ue_ref.at[2 * (ci % 2) + part]
            ue[...] = _dot(hext_ref[...], wup_ref[:, cs])
            conv.append(ue[HALO - 1:HALO - 1 + tm, :] * cw_ref[0:1, cs] + ue[HALO:HALO + tm, :] * cw_ref[1:2, cs]
                        + ue[HALO + 1:HALO + 1 + tm, :] * cw_ref[2:3, cs] + cb_ref[:, cs])
        a, g = conv
        act_ref[:, c0:c0 + fc] = (g * _sigmoid(g) * a).astype(BF16)
        if (ci + 1) % per_group == 0 or ci == n_chunks - 1:
            k0 = (ci // per_group) * per_group * fc
            part_f = _dot(act_ref[:, k0:c0 + fc], wdn_ref[k0:c0 + fc, :])
            f = part_f if f is None else f + part_f
    o_ref[...] = x + _rms(f, gpost_ref[...])


def _ffn(x2d, g_pre, w_up, conv_w, conv_b, w_down, g_post, batch, seq):
    m = batch * seq
    tm = min(ROW_TILE, seq)
    nb = seq // tm
    hb = tm // HALO
    last_halo = m // HALO - 1
    d_ff = w_down.shape[0]
    tile = pl.BlockSpec((tm, D_MODEL), lambda b, j: (b * nb + j, 0))
    return pl.pallas_call(
        functools.partial(_ffn_kernel, d_ff=d_ff, fc=FF_CHUNK, groups=FF_GROUPS),
        grid=(batch, nb),
        in_specs=[tile,
                  pl.BlockSpec((HALO, D_MODEL), lambda b, j: (jnp.maximum((b * nb + j) * hb - 1, 0), 0)),
                  pl.BlockSpec((HALO, D_MODEL), lambda b, j: (jnp.minimum((b * nb + j + 1) * hb, last_halo), 0)),
                  _const_spec((1, D_MODEL)), _const_spec(w_up.shape), _const_spec(conv_w.shape),
                  _const_spec((1, 2 * d_ff)), _const_spec(w_down.shape), _const_spec((1, D_MODEL))],
        out_specs=tile,
        out_shape=jax.ShapeDtypeStruct((m, D_MODEL), F32),
        scratch_shapes=[pltpu.VMEM((tm + 2 * HALO, D_MODEL), BF16),
                        pltpu.VMEM((4, tm + 2 * HALO, FF_CHUNK), F32),
                        pltpu.VMEM((tm, d_ff), BF16)],
        compiler_params=_params(("parallel", "parallel")),
        name="ffn",
    )(x2d, x2d, x2d, g_pre.reshape(1, D_MODEL), w_up, conv_w, conv_b.reshape(1, 2 * d_ff), w_down,
      g_post.reshape(1, D_MODEL))


def _encoder_stack(x, mem, g_pre_mix, g_post_mix, g_mem, a_w_main, a_w_gate, a_bias, a_g_head, a_w_out,
                   b_w_main, b_w_gate, b_lb, b_g_head, b_w_out, x_w_kv, g_pre_ffn, g_post_ffn,
                   f_w_up, f_conv_w, f_conv_b, f_w_down):
    batch, seq, d = x.shape
    x2d = x.reshape(batch * seq, d)
    mem2d = mem.reshape(batch * N_MEM, d)
    depth = g_pre_mix.shape[0]
    for i in range(depth):
        j = i // 2
        kv = _norm_proj(mem2d, g_mem[i], x_w_kv[i])
        if i % 2 == 0:
            p_main, gates = _norm_proj(x2d, g_pre_mix[i], a_w_main[j], a_w_gate[j])
            hf, hb = _mlstm_mixer(p_main, gates, a_bias[j], batch, seq)
            x2d = _mix_out(hf, hb, p_main, kv, a_w_out[j], a_g_head[j], g_post_mix[i], x2d, batch, seq,
                           head_dim=A_V, silu_gate=False)
        else:
            p_main, zgates = _norm_proj(x2d, g_pre_mix[i], b_w_main[j], b_w_gate[j])
            hf, hb = _hgrn_mixer(p_main, zgates, b_lb, j, batch, seq)
            x2d = _mix_out(hf, hb, p_main, kv, b_w_out[j], b_g_head[j], g_post_mix[i], x2d, batch, seq,
                           head_dim=B_V, silu_gate=True)
        x2d = _ffn(x2d, g_pre_ffn[i], f_w_up[i], f_conv_w[i], f_conv_b[i], f_w_down[i], g_post_ffn[i],
                   batch, seq)
    return x2d.reshape(batch, seq, d)


def kernel(x_prompt, x_sample, mem_prompt, mem_sample, g_pre_mix, g_post_mix, g_mem, a_w_in, a_b_gate, a_g_head, a_w_out, b_w_in, b_lb, b_g_head, b_w_out, x_w_kv, g_pre_ffn, g_post_ffn, f_w_up, f_conv_w, f_conv_b, f_w_down):
    a_rec = 2 * A_HEADS * A_QK + 2 * A_HEADS * A_V
    n_gate = 4 * A_HEADS
    a_w_main = jnp.concatenate([a_w_in[:, :, :a_rec], a_w_in[:, :, a_rec + n_gate:]], axis=-1).astype(BF16)
    perm_i = jnp.array([0, 1, 2, 3, 8, 9, 10, 11]) + a_rec
    perm_f = perm_i + A_HEADS

    def two_tiles(x):
        pad = [(0, 0)] * (x.ndim - 1) + [(0, LANES - 2 * A_HEADS)]
        return jnp.concatenate([jnp.pad(jnp.take(x, perm_i, axis=-1), pad),
                                jnp.pad(jnp.take(x, perm_f, axis=-1), pad)], axis=-1)

    a_w_gate = two_tiles(a_w_in).astype(BF16)
    a_bias = two_tiles(jnp.pad(a_b_gate, ((0, 0), (a_rec, 0)))).reshape(-1, 1, A_GATE_LANES)
    w = B_HEADS * B_F
    b_w_main = jnp.concatenate([b_w_in[:, :, :2 * w], b_w_in[:, :, 4 * w:]], axis=-1).astype(BF16)
    b_w_gate = b_w_in[:, :, 2 * w:4 * w].astype(BF16)
    shared = (g_pre_mix, g_post_mix, g_mem, a_w_main, a_w_gate, a_bias, a_g_head, a_w_out.astype(BF16),
              b_w_main, b_w_gate, b_lb, b_g_head, b_w_out.astype(BF16), x_w_kv.astype(BF16),
              g_pre_ffn, g_post_ffn, f_w_up.astype(BF16), f_conv_w, f_conv_b, f_w_down.astype(BF16))
    return (_encoder_stack(x_prompt, mem_prompt, *shared), _encoder_stack(x_sample, mem_sample, *shared))
```

```python
import functools

import jax
import jax.numpy as jnp
from jax import lax
from jax.experimental import pallas as pl
from jax.experimental.pallas import tpu as pltpu

F32 = jnp.float32
BF16 = jnp.bfloat16
EPS = 1e-6

D_MODEL = 1024
N_MEM = 256
X_HEADS = 4
X_HD = 128
X_W = X_HEADS * X_HD
A_HEADS = 4
A_QK = 128
A_V = 256
B_HEADS = 8
B_F = 128
B_V = 128

GATE_COL_BLOCK = 2
XQ_COL_BLOCK = 6
LANES = 128
SUBLANES = 8
A_GATE_LANES = 2 * LANES
A_VA = A_V + LANES

VMEM_LIMIT_BYTES = 56 * 1024 * 1024

ROW_TILE = 512
MIX_ROWS = 512
A_CHUNK = 128
B_CHUNK = 128
FF_CHUNK = 256
FF_GROUPS = 2
T_SLOTS = 8
LOG2E = 1.4426950408889634


def _params(sem):
    return pltpu.CompilerParams(dimension_semantics=sem, vmem_limit_bytes=VMEM_LIMIT_BYTES)


def _const_spec(shape):
    nd = len(shape)
    return pl.BlockSpec(shape, lambda *_: (0,) * nd, pipeline_mode=pl.Buffered(1))


def _rms(x, g):
    ms = jnp.mean(x * x, axis=-1, keepdims=True)
    return x * lax.rsqrt(ms + EPS) * g


def _sigmoid(x):
    return 1.0 / (1.0 + jnp.exp(-x))


def _log_sigmoid(x):
    return jnp.minimum(x, 0.0) - jnp.log(1.0 + jnp.exp(-jnp.abs(x)))


def _dot(a, b):
    return jnp.dot(a, b, preferred_element_type=F32)


def _dot_nt(a, b):
    return lax.dot_general(a, b, (((1,), (1,)), ((), ())), preferred_element_type=F32)


def _cumsum_rows(tri_bf, x):
    hi = x.astype(BF16)
    r1 = x - hi.astype(F32)
    mid = r1.astype(BF16)
    lo = (r1 - mid.astype(F32)).astype(BF16)
    return _dot(tri_bf, hi) + _dot(tri_bf, mid) + _dot(tri_bf, lo)


def _scan_rows(x, reverse, op):
    n = x.shape[0] // SUBLANES
    row8 = lax.broadcasted_iota(jnp.int32, (SUBLANES, 1), 0)
    ident = 0.0 if op is jnp.add else -jnp.inf
    parts = [x[i * SUBLANES:(i + 1) * SUBLANES] for i in range(n)]
    for s in (1, 2, 4):
        for i in range(n):
            p = parts[i]
            if not reverse:
                parts[i] = op(p, jnp.where(row8 >= s, pltpu.roll(p, s, 0), ident))
            else:
                parts[i] = op(p, jnp.where(row8 < SUBLANES - s, pltpu.roll(p, SUBLANES - s, 0), ident))
    carry = None
    for i in (range(n) if not reverse else range(n - 1, -1, -1)):
        if carry is not None:
            parts[i] = op(parts[i], carry)
        edge = parts[i][SUBLANES - 1:SUBLANES] if not reverse else parts[i][0:1]
        carry = jnp.broadcast_to(edge, (SUBLANES, x.shape[1]))
    return jnp.concatenate(parts, axis=0)


def _lane_bcast(x, c):
    return jnp.broadcast_to(x[:, c:c + 1], (x.shape[0], LANES))


def _norm_proj_kernel(x_ref, g_ref, wa_ref, *rest, col_chunk):
    if len(rest) == 3:
        wb_ref, oa_ref, ob_ref = rest
    else:
        (oa_ref,) = rest
        wb_ref = ob_ref = None
    h = _rms(x_ref[...], g_ref[...]).astype(BF16)
    for w_ref, o_ref in ((wa_ref, oa_ref), (wb_ref, ob_ref)):
        if w_ref is None:
            continue
        n = w_ref.shape[1]
        for c0 in range(0, n, col_chunk):
            cw = min(col_chunk, n - c0)
            o_ref[:, c0:c0 + cw] = _dot(h, w_ref[:, c0:c0 + cw]).astype(o_ref.dtype)


def _norm_proj(x2d, g, wa, wb=None):
    m, d = x2d.shape
    tm = min(ROW_TILE, m)
    in_specs = [pl.BlockSpec((tm, d), lambda i: (i, 0)), _const_spec((1, d)), _const_spec(wa.shape)]
    out_shape = [jax.ShapeDtypeStruct((m, wa.shape[1]), BF16)]
    out_specs = [pl.BlockSpec((tm, wa.shape[1]), lambda i: (i, 0))]
    args = [x2d, g.reshape(1, d), wa]
    if wb is not None:
        in_specs.append(_const_spec(wb.shape))
        out_shape.append(jax.ShapeDtypeStruct((m, wb.shape[1]), F32))
        out_specs.append(pl.BlockSpec((tm, wb.shape[1]), lambda i: (i, 0)))
        args.append(wb)
    out = pl.pallas_call(
        functools.partial(_norm_proj_kernel, col_chunk=512),
        grid=(m // tm,),
        in_specs=in_specs,
        out_specs=out_specs,
        out_shape=out_shape,
        compiler_params=_params(("parallel",)),
        name="norm_proj",
    )(*args)
    return out if wb is not None else out[0]


def _mlstm_kernel(qf_ref, kf_ref, vf_ref, gf_ref, qb_ref, kb_ref, vb_ref, gb_ref, bias_ref,
                  hf_ref, hb_ref, c_ref, m_ref, *, chunk):
    L = chunk
    nch = qf_ref.shape[0] // L
    scale = A_QK ** -0.5

    @pl.when(pl.program_id(1) == 0)
    def _():
        c_ref[...] = jnp.zeros_like(c_ref)
        m_ref[...] = jnp.zeros_like(m_ref)

    ti = lax.broadcasted_iota(jnp.int32, (L, L), 0)
    si = lax.broadcasted_iota(jnp.int32, (L, L), 1)
    masks = (si <= ti, si >= ti)
    tris = (masks[0].astype(BF16), masks[1].astype(BF16))
    bias_i = bias_ref[:, 0:LANES]
    bias_f = bias_ref[:, LANES:2 * LANES]
    ones = jnp.ones((L, LANES), BF16)
    views = ((qf_ref, kf_ref, vf_ref, gf_ref, hf_ref), (qb_ref, kb_ref, vb_ref, gb_ref, hb_ref))

    for idx in range(nch):
        for d in range(2):
            q_ref, k_ref, v_ref, g_ref, h_ref = views[d]
            c0 = (idx if d == 0 else nch - 1 - idx) * L
            rows = slice(c0, c0 + L)
            li = g_ref[rows, 0:LANES] + bias_i
            cum = _cumsum_rows(tris[d], _log_sigmoid(g_ref[rows, LANES:2 * LANES] + bias_f))
            u = li - cum
            cm = _scan_rows(u, d == 1, jnp.maximum)
            u_t = u.T
            for hd in range(A_HEADS):
                chain = d * A_HEADS + hd
                b_bc = _lane_bcast(cum, chain)
                cm_bc = _lane_bcast(cm, chain)
                u_row = u_t[chain:chain + 1, :]
                bl = b_bc[L - 1:L, :] if d == 0 else b_bc[0:1, :]
                q = q_ref[rows, hd * A_QK:(hd + 1) * A_QK]
                k = k_ref[rows, hd * A_QK:(hd + 1) * A_QK]
                v_aug = jnp.concatenate([v_ref[rows, hd * A_V:(hd + 1) * A_V], ones], axis=1)
                s = _dot_nt(q, k) * jnp.where(masks[d], jnp.exp(u_row - cm_bc), 0.0)
                g_row = u_row + bl
                g_max = jnp.max(g_row, axis=1, keepdims=True)
                kw_t = (k.T.astype(F32) * jnp.exp(g_row - g_max)).astype(BF16)
                dca = _dot(kw_t, v_aug)
                m_old = m_ref[chain]
                c_old = c_ref[chain]
                mx = jnp.maximum(m_old, cm_bc)
                a_loc = scale * jnp.exp(cm_bc - mx)
                a_int = scale * jnp.exp(m_old - mx)
                lhs = jnp.concatenate([(a_loc * s).astype(BF16), (a_int * q.astype(F32)).astype(BF16)], axis=1)
                nd = _dot(lhs, jnp.concatenate([v_aug, c_old.astype(BF16)], axis=0))
                rden = 1.0 / jnp.maximum(jnp.abs(nd[:, A_V:]), jnp.exp(-(b_bc + mx)))
                for c1 in range(0, A_V, LANES):
                    h_ref[rows, hd * A_V + c1:hd * A_V + c1 + LANES] = (nd[:, c1:c1 + LANES] * rden).astype(h_ref.dtype)
                g_max_b = jnp.broadcast_to(g_max, (1, LANES))
                m_new = jnp.maximum(bl + m_old, g_max_b)
                dec = jnp.exp(bl + m_old - m_new)
                inc = jnp.exp(g_max_b - m_new)
                n_tiles = A_VA // LANES
                c_ref[chain] = (jnp.concatenate([dec] * n_tiles, axis=1) * c_old
                                + jnp.concatenate([inc] * n_tiles, axis=1) * dca)
                m_ref[chain] = m_new


def _mlstm_mixer(p_main, gates, bias, batch, seq):
    m = batch * seq
    rows = min(MIX_ROWS, seq)
    nb = seq // rows
    fwd = lambda b, j: b * nb + j
    bwd = lambda b, j: b * nb + nb - 1 - j

    def specs(row):
        return [pl.BlockSpec((rows, 512), lambda b, j: (row(b, j), 0)),
                pl.BlockSpec((rows, 512), lambda b, j: (row(b, j), 1)),
                pl.BlockSpec((rows, 1024), lambda b, j: (row(b, j), 1)),
                pl.BlockSpec((rows, A_GATE_LANES), lambda b, j: (row(b, j), 0))]

    n_chain = 2 * A_HEADS
    return pl.pallas_call(
        functools.partial(_mlstm_kernel, chunk=A_CHUNK),
        grid=(batch, nb),
        in_specs=specs(fwd) + specs(bwd) + [_const_spec((1, A_GATE_LANES))],
        out_specs=[pl.BlockSpec((rows, D_MODEL), lambda b, j: (fwd(b, j), 0)),
                   pl.BlockSpec((rows, D_MODEL), lambda b, j: (bwd(b, j), 0))],
        out_shape=[jax.ShapeDtypeStruct((m, D_MODEL), BF16)] * 2,
        scratch_shapes=[pltpu.VMEM((n_chain, A_QK, A_VA), F32),
                        pltpu.VMEM((n_chain, 1, LANES), F32)],
        compiler_params=_params(("parallel", "arbitrary")),
        name="mlstm_mixer",
    )(p_main, p_main, p_main, gates, p_main, p_main, p_main, gates, bias)


def _hgrn_kernel(qf_ref, vf_ref, zf_ref, qb_ref, vb_ref, zb_ref, lb_ref, of_ref, ob_ref, st_ref, t_scr,
                 *, chunk, layer):
    L = chunk
    nch = qf_ref.shape[0] // L
    W = B_HEADS * B_F

    @pl.when(pl.program_id(1) == 0)
    def _():
        st_ref[...] = jnp.zeros_like(st_ref)

    n_slots = t_scr.shape[0]
    uses = [0]

    def dot_nt(a, b):
        slot = t_scr.at[uses[0] % n_slots]
        uses[0] += 1
        slot[...] = b.T
        return _dot(a, slot[...])

    row8 = lax.broadcasted_iota(jnp.int32, (SUBLANES, 1), 0)
    n_sub = L // SUBLANES
    k_sub = SUBLANES.bit_length() - 1
    small_ks = list(range(k_sub))
    big_ks = list(range(k_sub, L.bit_length() - 1))

    shr = lax.shift_right_logical
    row_i = lax.broadcasted_iota(jnp.int32, (SUBLANES, L), 0)
    lane_i = lax.broadcasted_iota(jnp.int32, (SUBLANES, L), 1)
    lane_in = lane_i & (SUBLANES - 1)
    eye_pat = row_i == lane_in
    lane_grp = {k: shr(lane_i, k) for k in big_ks}

    def small_pattern(k, d):
        same_pair = shr(row_i, k + 1) == shr(lane_in, k + 1)
        t_side = (shr(row_i, k) & 1) == (1 if d == 0 else 0)
        s_side = (shr(lane_in, k) & 1) == (0 if d == 0 else 1)
        return same_pair & t_side & s_side

    pats = [[small_pattern(k, d) for k in small_ks] for d in range(2)]

    lbs = []
    for d in range(2):
        p = lb_ref[d]
        e = jnp.exp(p - jnp.max(p, axis=0, keepdims=True))
        sm = e / jnp.sum(e, axis=0, keepdims=True)
        lb = jnp.zeros((1, W), F32)
        for r in range(1, layer + 1):
            lb = lb + sm[r:r + 1, :]
        lbs.append(lb)
    views = ((qf_ref, vf_ref, zf_ref, of_ref), (qb_ref, vb_ref, zb_ref, ob_ref))

    def boundary_rows(b, c, d):
        off = c - 1 if d == 0 else c
        if 2 * c >= SUBLANES:
            parts = [jnp.broadcast_to(b[p0 + off:p0 + off + 1, :], (2 * c, B_F)) for p0 in range(0, L, 2 * c)]
        else:
            parts = [jnp.where(row8 < 4,
                               jnp.broadcast_to(b[g + off:g + off + 1, :], (SUBLANES, B_F)),
                               jnp.broadcast_to(b[g + 4 + off:g + 5 + off, :], (SUBLANES, B_F)))
                     for g in range(0, L, SUBLANES)]
        return parts[0] if len(parts) == 1 else jnp.concatenate(parts, axis=0)

    for idx in range(nch):
        for d in range(2):
            q_ref, v_ref, z_ref, o_ref = views[d]
            c0 = (idx if d == 0 else nch - 1 - idx) * L
            rows = slice(c0, c0 + L)
            for h in range(B_HEADS):
                chain = d * B_HEADS + h
                fs = slice(h * B_F, (h + 1) * B_F)
                vs = slice(h * B_V, (h + 1) * B_V)
                lb = lbs[d][:, fs]
                th = jnp.tanh(0.5 * z_ref[rows, fs])
                f = lb + (1.0 - lb) * (0.5 + 0.5 * th)
                kk = (1.0 - lb) * (0.5 - 0.5 * th)
                b = _scan_rows(jnp.log(f), d == 1, jnp.add)
                bl = b[L - 1:L, :] if d == 0 else b[0:1, :]
                b2 = b * LOG2E
                q_bf = q_ref[rows, fs]
                q = q_bf.astype(F32)
                v_bf = v_ref[rows, vs]
                t_upper = d == 0

                def level_product(k):
                    c = 1 << k
                    if c >= SUBLANES:
                        lo_src, hi_src = (kk, q) if t_upper else (q, kk)
                        parts = []
                        for p0 in range(0, L, 2 * c):
                            parts += [lo_src[p0:p0 + c], hi_src[p0 + c:p0 + 2 * c]]
                        x = jnp.concatenate(parts, axis=0)
                    else:
                        is_upper = (shr(row8, k) & 1) == 1
                        is_t = is_upper if t_upper else jnp.logical_not(is_upper)
                        qq = q * f if c == 1 else q
                        x = jnp.concatenate([jnp.where(is_t, qq[g:g + SUBLANES], kk[g:g + SUBLANES])
                                             for g in range(0, L, SUBLANES)], axis=0)
                    if c > 1:
                        x = x * jnp.exp2(-jnp.abs(b2 - boundary_rows(b2, c, d)))
                    w = x.astype(BF16)
                    return dot_nt(w, w)

                def row_group(x, j):
                    return x[j * SUBLANES:(j + 1) * SUBLANES]

                prod = dot_nt(q_bf, kk.astype(BF16))
                acc = [jnp.where(eye_pat, row_group(prod, j), 0.0) for j in range(n_sub)]
                for k in small_ks:
                    prod = level_product(k)
                    acc = [jnp.where(pats[d][k], row_group(prod, j), acc[j]) for j in range(n_sub)]
                acc = [jnp.where(lane_grp[k_sub] == j, acc[j], 0.0) for j in range(n_sub)]
                for k in big_ks:
                    prod = level_product(k)
                    for j in range(n_sub):
                        grp = (j * SUBLANES) >> k
                        if (grp & 1) == (1 if t_upper else 0):
                            sib = grp - 1 if t_upper else grp + 1
                            acc[j] = jnp.where(lane_grp[k] == sib, row_group(prod, j), acc[j])
                a = jnp.concatenate(acc, axis=0)
                st = st_ref[chain]
                qe = (q * jnp.exp2(b2)).astype(BF16)
                ke = (kk * jnp.exp(bl - b)).astype(BF16)
                o_ref[rows, vs] = (dot_nt(qe, st.astype(BF16)) + _dot(a.astype(BF16), v_bf)).astype(o_ref.dtype)
                st_ref[chain] = st * jnp.exp(bl) + lax.dot_general(
                    v_bf, ke, (((0,), (0,)), ((), ())), preferred_element_type=F32)


def _hgrn_mixer(p_main, zgates, b_lb, layer, batch, seq):
    m = batch * seq
    rows = min(MIX_ROWS, seq)
    nb = seq // rows
    fwd = lambda b, j: b * nb + j
    bwd = lambda b, j: b * nb + nb - 1 - j

    def specs(row, zcol):
        return [pl.BlockSpec((rows, 1024), lambda b, j: (row(b, j), 0)),
                pl.BlockSpec((rows, 1024), lambda b, j: (row(b, j), 1)),
                pl.BlockSpec((rows, 1024), lambda b, j: (row(b, j), zcol))]

    return pl.pallas_call(
        functools.partial(_hgrn_kernel, chunk=B_CHUNK, layer=layer),
        grid=(batch, nb),
        in_specs=specs(fwd, 0) + specs(bwd, 1) + [_const_spec(b_lb.shape)],
        out_specs=[pl.BlockSpec((rows, D_MODEL), lambda b, j: (fwd(b, j), 0)),
                   pl.BlockSpec((rows, D_MODEL), lambda b, j: (bwd(b, j), 0))],
        out_shape=[jax.ShapeDtypeStruct((m, D_MODEL), BF16)] * 2,
        scratch_shapes=[pltpu.VMEM((2 * B_HEADS, B_V, B_F), F32),
                        pltpu.VMEM((T_SLOTS, B_F, B_CHUNK), BF16)],
        compiler_params=_params(("parallel", "arbitrary")),
        name="hgrn_mixer",
    )(p_main, p_main, zgates, p_main, p_main, zgates, b_lb)


def _mix_out_kernel(hf_ref, hb_ref, gate_ref, xq_ref, kv_ref, wo_ref, gh_ref, gp_ref, x_ref, o_ref,
                    *, head_dim, silu_gate):
    h = hf_ref[...].astype(F32) + hb_ref[...].astype(F32)
    parts = []
    for c0 in range(0, D_MODEL, head_dim):
        hh = h[:, c0:c0 + head_dim]
        parts.append(hh * lax.rsqrt(jnp.mean(hh * hh, axis=-1, keepdims=True) + EPS))
    gate = gate_ref[...].astype(F32)
    gate = gate * _sigmoid(gate) if silu_gate else _sigmoid(gate)
    mix = (jnp.concatenate(parts, axis=-1) * gh_ref[...] * gate).astype(BF16)
    y = _dot(mix, wo_ref[0:D_MODEL, :])

    xa = []
    for hd in range(X_HEADS):
        cs = slice(hd * X_HD, (hd + 1) * X_HD)
        s = _dot_nt(xq_ref[:, cs], kv_ref[:, cs]) * (X_HD ** -0.5)
        e = jnp.exp(s - jnp.max(s, axis=-1, keepdims=True))
        pv = _dot(e.astype(BF16), kv_ref[:, X_W + hd * X_HD:X_W + (hd + 1) * X_HD])
        xa.append((pv / jnp.sum(e, axis=-1, keepdims=True)).astype(BF16))
    y = y + _dot(jnp.concatenate(xa, axis=-1), wo_ref[D_MODEL:D_MODEL + X_W, :])
    o_ref[...] = x_ref[...] + _rms(y, gp_ref[...])


def _mix_out(hf, hb, p_main, kv, w_out, g_head, g_post, x2d, batch, seq, head_dim, silu_gate):
    m = batch * seq
    tm = min(ROW_TILE, seq)
    nb = seq // tm
    row = lambda b, j: b * nb + j
    tile = pl.BlockSpec((tm, D_MODEL), lambda b, j: (row(b, j), 0))
    return pl.pallas_call(
        functools.partial(_mix_out_kernel, head_dim=head_dim, silu_gate=silu_gate),
        grid=(batch, nb),
        in_specs=[tile, tile,
                  pl.BlockSpec((tm, D_MODEL), lambda b, j: (row(b, j), GATE_COL_BLOCK)),
                  pl.BlockSpec((tm, X_W), lambda b, j: (row(b, j), XQ_COL_BLOCK)),
                  pl.BlockSpec((N_MEM, 2 * X_W), lambda b, j: (b, 0)),
                  _const_spec(w_out.shape), _const_spec((1, D_MODEL)), _const_spec((1, D_MODEL)),
                  tile],
        out_specs=tile,
        out_shape=jax.ShapeDtypeStruct((m, D_MODEL), F32),
        compiler_params=_params(("parallel", "parallel")),
        name="mix_out",
    )(hf, hb, p_main, p_main, kv, w_out, g_head.reshape(1, D_MODEL), g_post.reshape(1, D_MODEL), x2d)


HALO = 16


def _ffn_kernel(x_ref, xp_ref, xn_ref, gpre_ref, wup_ref, cw_ref, cb_ref, wdn_ref, gpost_ref, o_ref,
                hext_ref, act_ref, *, d_ff, fc, groups):
    tm = x_ref.shape[0]
    j = pl.program_id(1)
    gpre = gpre_ref[...]
    x = x_ref[...]
    has_prev = (j > 0).astype(F32)
    has_next = (j < pl.num_programs(1) - 1).astype(F32)
    hext_ref[0:HALO, :] = (_rms(xp_ref[...], gpre) * has_prev).astype(BF16)
    hext_ref[HALO:HALO + tm, :] = _rms(x, gpre).astype(BF16)
    hext_ref[HALO + tm:HALO + tm + HALO, :] = (_rms(xn_ref[...], gpre) * has_next).astype(BF16)

    starts = list(range(0, d_ff, fc))
    per_group = -(-len(starts) // groups)
    n_ext = tm + 2 * HALO
    f = None
    for ci, c0 in enumerate(starts):
        cw = min(fc, d_ff - c0)
        conv = []
        for base in (c0, d_ff + c0):
            cs = slice(base, base + cw)
            ue = _dot(hext_ref[...], wup_ref[:, cs])
            u_prev = pltpu.roll(ue, 1, 0)[HALO:HALO + tm]
            u_next = pltpu.roll(ue, n_ext - 1, 0)[HALO:HALO + tm]
            conv.append(u_prev * cw_ref[0:1, cs] + ue[HALO:HALO + tm] * cw_ref[1:2, cs]
                        + u_next * cw_ref[2:3, cs] + cb_ref[:, cs])
        a, g = conv
        act_ref[:, c0:c0 + cw] = (g * _sigmoid(g) * a).astype(BF16)
        if (ci + 1) % per_group == 0 or ci == len(starts) - 1:
            k0 = starts[(ci // per_group) * per_group]
            part_f = _dot(act_ref[:, k0:c0 + cw], wdn_ref[k0:c0 + cw, :])
            f = part_f if f is None else f + part_f
    o_ref[...] = x + _rms(f, gpost_ref[...])


def _ffn(x2d, g_pre, w_up, conv_w, conv_b, w_down, g_post, batch, seq):
    m = batch * seq
    tm = min(ROW_TILE, seq)
    nb = seq // tm
    hb = tm // HALO
    last_halo = m // HALO - 1
    d_ff = w_down.shape[0]
    tile = pl.BlockSpec((tm, D_MODEL), lambda b, j: (b * nb + j, 0))
    return pl.pallas_call(
        functools.partial(_ffn_kernel, d_ff=d_ff, fc=FF_CHUNK, groups=FF_GROUPS),
        grid=(batch, nb),
        in_specs=[tile,
                  pl.BlockSpec((HALO, D_MODEL), lambda b, j: (jnp.maximum((b * nb + j) * hb - 1, 0), 0)),
                  pl.BlockSpec((HALO, D_MODEL), lambda b, j: (jnp.minimum((b * nb + j + 1) * hb, last_halo), 0)),
                  _const_spec((1, D_MODEL)), _const_spec(w_up.shape), _const_spec(conv_w.shape),
                  _const_spec((1, 2 * d_ff)), _const_spec(w_down.shape), _const_spec((1, D_MODEL))],
        out_specs=tile,
        out_shape=jax.ShapeDtypeStruct((m, D_MODEL), F32),
        scratch_shapes=[pltpu.VMEM((tm + 2 * HALO, D_MODEL), BF16),
                        pltpu.VMEM((tm, d_ff), BF16)],
        compiler_params=_params(("parallel", "parallel")),
        name="ffn",
    )(x2d, x2d, x2d, g_pre.reshape(1, D_MODEL), w_up, conv_w, conv_b.reshape(1, 2 * d_ff), w_down,
      g_post.reshape(1, D_MODEL))


def _encoder_stack(x, mem, g_pre_mix, g_post_mix, g_mem, a_w_main, a_w_gate, a_bias, a_g_head, a_w_out,
                   b_w_main, b_w_gate, b_lb, b_g_head, b_w_out, x_w_kv, g_pre_ffn, g_post_ffn,
                   f_w_up, f_conv_w, f_conv_b, f_w_down):
    batch, seq, d = x.shape
    x2d = x.reshape(batch * seq, d)
    mem2d = mem.reshape(batch * N_MEM, d)
    depth = g_pre_mix.shape[0]
    for i in range(depth):
        j = i // 2
        kv = _norm_proj(mem2d, g_mem[i], x_w_kv[i])
        if i % 2 == 0:
            p_main, gates = _norm_proj(x2d, g_pre_mix[i], a_w_main[j], a_w_gate[j])
            hf, hb = _mlstm_mixer(p_main, gates, a_bias[j], batch, seq)
            x2d = _mix_out(hf, hb, p_main, kv, a_w_out[j], a_g_head[j], g_post_mix[i], x2d, batch, seq,
                           head_dim=A_V, silu_gate=False)
        else:
            p_main, zgates = _norm_proj(x2d, g_pre_mix[i], b_w_main[j], b_w_gate[j])
            hf, hb = _hgrn_mixer(p_main, zgates, b_lb, j, batch, seq)
            x2d = _mix_out(hf, hb, p_main, kv, b_w_out[j], b_g_head[j], g_post_mix[i], x2d, batch, seq,
                           head_dim=B_V, silu_gate=True)
        x2d = _ffn(x2d, g_pre_ffn[i], f_w_up[i], f_conv_w[i], f_conv_b[i], f_w_down[i], g_post_ffn[i],
                   batch, seq)
    return x2d.reshape(batch, seq, d)


def kernel(x_prompt, x_sample, mem_prompt, mem_sample, g_pre_mix, g_post_mix, g_mem, a_w_in, a_b_gate, a_g_head, a_w_out, b_w_in, b_lb, b_g_head, b_w_out, x_w_kv, g_pre_ffn, g_post_ffn, f_w_up, f_conv_w, f_conv_b, f_w_down):
    a_rec = 2 * A_HEADS * A_QK + 2 * A_HEADS * A_V
    n_gate = 4 * A_HEADS
    a_w_main = jnp.concatenate([a_w_in[:, :, :a_rec], a_w_in[:, :, a_rec + n_gate:]], axis=-1).astype(BF16)
    perm_i = jnp.array([0, 1, 2, 3, 8, 9, 10, 11]) + a_rec
    perm_f = perm_i + A_HEADS

    def two_tiles(x):
        pad = [(0, 0)] * (x.ndim - 1) + [(0, LANES - 2 * A_HEADS)]
        return jnp.concatenate([jnp.pad(jnp.take(x, perm_i, axis=-1), pad),
                                jnp.pad(jnp.take(x, perm_f, axis=-1), pad)], axis=-1)

    a_w_gate = two_tiles(a_w_in).astype(BF16)
    a_bias = two_tiles(jnp.pad(a_b_gate, ((0, 0), (a_rec, 0)))).reshape(-1, 1, A_GATE_LANES)
    w = B_HEADS * B_F
    b_w_main = jnp.concatenate([b_w_in[:, :, :2 * w], b_w_in[:, :, 4 * w:]], axis=-1).astype(BF16)
    b_w_gate = b_w_in[:, :, 2 * w:4 * w].astype(BF16)
    shared = (g_pre_mix, g_post_mix, g_mem, a_w_main, a_w_gate, a_bias, a_g_head, a_w_out.astype(BF16),
              b_w_main, b_w_gate, b_lb, b_g_head, b_w_out.astype(BF16), x_w_kv.astype(BF16),
              g_pre_ffn, g_post_ffn, f_w_up.astype(BF16), f_conv_w, f_conv_b, f_w_down.astype(BF16))
    return (_encoder_stack(x_prompt, mem_prompt, *shared), _encoder_stack(x_sample, mem_sample, *shared))
```

```python
import functools

import jax
import jax.numpy as jnp
from jax import lax
from jax.experimental import pallas as pl
from jax.experimental.pallas import tpu as pltpu

F32 = jnp.float32
BF16 = jnp.bfloat16
EPS = 1e-6

D_MODEL = 1024
N_MEM = 256
X_HEADS = 4
X_HD = 128
X_W = X_HEADS * X_HD
A_HEADS = 4
A_QK = 128
A_V = 256
B_HEADS = 8
B_F = 128
B_V = 128

GATE_COL_BLOCK = 2
XQ_COL_BLOCK = 6
LANES = 128
SUBLANES = 8
A_GATE_LANES = 2 * LANES
A_VA = A_V + LANES

VMEM_LIMIT_BYTES = 56 * 1024 * 1024

ROW_TILE = 512
FFN_TILE = 1024
MIX_ROWS = 512
A_CHUNK = 128
B_CHUNK = 128
FF_CHUNK = 256
FF_GROUPS = 2
T_SLOTS = 8


def _params(sem):
    return pltpu.CompilerParams(dimension_semantics=sem, vmem_limit_bytes=VMEM_LIMIT_BYTES)


def _const_spec(shape):
    nd = len(shape)
    return pl.BlockSpec(shape, lambda *_: (0,) * nd, pipeline_mode=pl.Buffered(1))


def _rms(x, g):
    ms = jnp.mean(x * x, axis=-1, keepdims=True)
    return x * lax.rsqrt(ms + EPS) * g


def _sigmoid(x):
    return 1.0 / (1.0 + jnp.exp(-x))


def _log_sigmoid(x):
    return jnp.minimum(x, 0.0) - jnp.log(1.0 + jnp.exp(-jnp.abs(x)))


def _dot(a, b):
    return jnp.dot(a, b, preferred_element_type=F32)


def _dot_nt(a, b):
    return lax.dot_general(a, b, (((1,), (1,)), ((), ())), preferred_element_type=F32)


def _cumsum_rows(tri_bf, x):
    hi = x.astype(BF16)
    r1 = x - hi.astype(F32)
    mid = r1.astype(BF16)
    lo = (r1 - mid.astype(F32)).astype(BF16)
    return _dot(tri_bf, hi) + _dot(tri_bf, mid) + _dot(tri_bf, lo)


def _scan_rows(x, reverse, op):
    n = x.shape[0] // SUBLANES
    row8 = lax.broadcasted_iota(jnp.int32, (SUBLANES, 1), 0)
    ident = 0.0 if op is jnp.add else -jnp.inf
    parts = [x[i * SUBLANES:(i + 1) * SUBLANES] for i in range(n)]
    for s in (1, 2, 4):
        for i in range(n):
            p = parts[i]
            if not reverse:
                parts[i] = op(p, jnp.where(row8 >= s, pltpu.roll(p, s, 0), ident))
            else:
                parts[i] = op(p, jnp.where(row8 < SUBLANES - s, pltpu.roll(p, SUBLANES - s, 0), ident))
    carry = None
    for i in (range(n) if not reverse else range(n - 1, -1, -1)):
        if carry is not None:
            parts[i] = op(parts[i], carry)
        edge = parts[i][SUBLANES - 1:SUBLANES] if not reverse else parts[i][0:1]
        carry = jnp.broadcast_to(edge, (SUBLANES, x.shape[1]))
    return jnp.concatenate(parts, axis=0)


def _lane_bcast(x, c):
    return jnp.broadcast_to(x[:, c:c + 1], (x.shape[0], LANES))


def _norm_proj_kernel(x_ref, g_ref, wa_ref, *rest, col_chunk):
    if len(rest) == 3:
        wb_ref, oa_ref, ob_ref = rest
    else:
        (oa_ref,) = rest
        wb_ref = ob_ref = None
    h = _rms(x_ref[...], g_ref[...]).astype(BF16)
    for w_ref, o_ref in ((wa_ref, oa_ref), (wb_ref, ob_ref)):
        if w_ref is None:
            continue
        n = w_ref.shape[1]
        for c0 in range(0, n, col_chunk):
            cw = min(col_chunk, n - c0)
            o_ref[:, c0:c0 + cw] = _dot(h, w_ref[:, c0:c0 + cw]).astype(o_ref.dtype)


def _norm_proj(x2d, g, wa, wb=None):
    m, d = x2d.shape
    tm = min(ROW_TILE, m)
    in_specs = [pl.BlockSpec((tm, d), lambda i: (i, 0)), _const_spec((1, d)), _const_spec(wa.shape)]
    out_shape = [jax.ShapeDtypeStruct((m, wa.shape[1]), BF16)]
    out_specs = [pl.BlockSpec((tm, wa.shape[1]), lambda i: (i, 0))]
    args = [x2d, g.reshape(1, d), wa]
    if wb is not None:
        in_specs.append(_const_spec(wb.shape))
        out_shape.append(jax.ShapeDtypeStruct((m, wb.shape[1]), F32))
        out_specs.append(pl.BlockSpec((tm, wb.shape[1]), lambda i: (i, 0)))
        args.append(wb)
    out = pl.pallas_call(
        functools.partial(_norm_proj_kernel, col_chunk=512),
        grid=(m // tm,),
        in_specs=in_specs,
        out_specs=out_specs,
        out_shape=out_shape,
        compiler_params=_params(("parallel",)),
        name="norm_proj",
    )(*args)
    return out if wb is not None else out[0]


def _mlstm_kernel(qf_ref, kf_ref, vf_ref, gf_ref, qb_ref, kb_ref, vb_ref, gb_ref, bias_ref,
                  hf_ref, hb_ref, c_ref, m_ref, *, chunk):
    L = chunk
    nch = qf_ref.shape[0] // L
    scale = A_QK ** -0.5

    @pl.when(pl.program_id(1) == 0)
    def _():
        c_ref[...] = jnp.zeros_like(c_ref)
        m_ref[...] = jnp.zeros_like(m_ref)

    ti = lax.broadcasted_iota(jnp.int32, (L, L), 0)
    si = lax.broadcasted_iota(jnp.int32, (L, L), 1)
    masks = (si <= ti, si >= ti)
    tris = (masks[0].astype(BF16), masks[1].astype(BF16))
    bias_i = bias_ref[:, 0:LANES]
    bias_f = bias_ref[:, LANES:2 * LANES]
    ones = jnp.ones((L, LANES), BF16)
    views = ((qf_ref, kf_ref, vf_ref, gf_ref, hf_ref), (qb_ref, kb_ref, vb_ref, gb_ref, hb_ref))

    for idx in range(nch):
        for d in range(2):
            q_ref, k_ref, v_ref, g_ref, h_ref = views[d]
            c0 = (idx if d == 0 else nch - 1 - idx) * L
            rows = slice(c0, c0 + L)
            li = g_ref[rows, 0:LANES] + bias_i
            cum = _cumsum_rows(tris[d], _log_sigmoid(g_ref[rows, LANES:2 * LANES] + bias_f))
            u = li - cum
            cm = _scan_rows(u, d == 1, jnp.maximum)
            u_t = u.T
            for hd in range(A_HEADS):
                chain = d * A_HEADS + hd
                b_bc = _lane_bcast(cum, chain)
                cm_bc = _lane_bcast(cm, chain)
                u_row = u_t[chain:chain + 1, :]
                bl = b_bc[L - 1:L, :] if d == 0 else b_bc[0:1, :]
                q = q_ref[rows, hd * A_QK:(hd + 1) * A_QK]
                k = k_ref[rows, hd * A_QK:(hd + 1) * A_QK]
                v_aug = jnp.concatenate([v_ref[rows, hd * A_V:(hd + 1) * A_V], ones], axis=1)
                s = _dot_nt(q, k) * jnp.where(masks[d], jnp.exp(u_row - cm_bc), 0.0)
                g_row = u_row + bl
                g_max = jnp.max(g_row, axis=1, keepdims=True)
                kw_t = (k.T.astype(F32) * jnp.exp(g_row - g_max)).astype(BF16)
                dca = _dot(kw_t, v_aug)
                m_old = m_ref[chain]
                c_old = c_ref[chain]
                mx = jnp.maximum(m_old, cm_bc)
                a_loc = scale * jnp.exp(cm_bc - mx)
                a_int = scale * jnp.exp(m_old - mx)
                lhs = jnp.concatenate([(a_loc * s).astype(BF16), (a_int * q.astype(F32)).astype(BF16)], axis=1)
                nd = _dot(lhs, jnp.concatenate([v_aug, c_old.astype(BF16)], axis=0))
                rden = 1.0 / jnp.maximum(jnp.abs(nd[:, A_V:]), jnp.exp(-(b_bc + mx)))
                for c1 in range(0, A_V, LANES):
                    h_ref[rows, hd * A_V + c1:hd * A_V + c1 + LANES] = (nd[:, c1:c1 + LANES] * rden).astype(h_ref.dtype)
                g_max_b = jnp.broadcast_to(g_max, (1, LANES))
                m_new = jnp.maximum(bl + m_old, g_max_b)
                dec = jnp.exp(bl + m_old - m_new)
                inc = jnp.exp(g_max_b - m_new)
                n_tiles = A_VA // LANES
                c_ref[chain] = (jnp.concatenate([dec] * n_tiles, axis=1) * c_old
                                + jnp.concatenate([inc] * n_tiles, axis=1) * dca)
                m_ref[chain] = m_new


def _mlstm_mixer(p_main, gates, bias, batch, seq):
    m = batch * seq
    rows = min(MIX_ROWS, seq)
    nb = seq // rows
    fwd = lambda b, j: b * nb + j
    bwd = lambda b, j: b * nb + nb - 1 - j

    def specs(row):
        return [pl.BlockSpec((rows, 512), lambda b, j: (row(b, j), 0)),
                pl.BlockSpec((rows, 512), lambda b, j: (row(b, j), 1)),
                pl.BlockSpec((rows, 1024), lambda b, j: (row(b, j), 1)),
                pl.BlockSpec((rows, A_GATE_LANES), lambda b, j: (row(b, j), 0))]

    n_chain = 2 * A_HEADS
    return pl.pallas_call(
        functools.partial(_mlstm_kernel, chunk=A_CHUNK),
        grid=(batch, nb),
        in_specs=specs(fwd) + specs(bwd) + [_const_spec((1, A_GATE_LANES))],
        out_specs=[pl.BlockSpec((rows, D_MODEL), lambda b, j: (fwd(b, j), 0)),
                   pl.BlockSpec((rows, D_MODEL), lambda b, j: (bwd(b, j), 0))],
        out_shape=[jax.ShapeDtypeStruct((m, D_MODEL), BF16)] * 2,
        scratch_shapes=[pltpu.VMEM((n_chain, A_QK, A_VA), F32),
                        pltpu.VMEM((n_chain, 1, LANES), F32)],
        compiler_params=_params(("parallel", "arbitrary")),
        name="mlstm_mixer",
    )(p_main, p_main, p_main, gates, p_main, p_main, p_main, gates, bias)


def _hgrn_kernel(qf_ref, vf_ref, zf_ref, qb_ref, vb_ref, zb_ref, lb_ref, of_ref, ob_ref, st_ref, t_scr,
                 *, chunk, layer):
    L = chunk
    nch = qf_ref.shape[0] // L
    W = B_HEADS * B_F

    @pl.when(pl.program_id(1) == 0)
    def _():
        st_ref[...] = jnp.zeros_like(st_ref)

    n_slots = t_scr.shape[0]
    uses = [0]

    def dot_nt(a, b):
        slot = t_scr.at[uses[0] % n_slots]
        uses[0] += 1
        slot[...] = b.T
        return _dot(a, slot[...])

    row8 = lax.broadcasted_iota(jnp.int32, (SUBLANES, 1), 0)
    n_sub = L // SUBLANES
    k_sub = SUBLANES.bit_length() - 1
    small_ks = list(range(k_sub))
    big_ks = list(range(k_sub, L.bit_length() - 1))

    shr = lax.shift_right_logical
    row_i = lax.broadcasted_iota(jnp.int32, (SUBLANES, L), 0)
    lane_i = lax.broadcasted_iota(jnp.int32, (SUBLANES, L), 1)
    lane_in = lane_i & (SUBLANES - 1)
    eye_pat = row_i == lane_in
    lane_grp = {k: shr(lane_i, k) for k in big_ks}

    def small_pattern(k, d):
        same_pair = shr(row_i, k + 1) == shr(lane_in, k + 1)
        t_side = (shr(row_i, k) & 1) == (1 if d == 0 else 0)
        s_side = (shr(lane_in, k) & 1) == (0 if d == 0 else 1)
        return same_pair & t_side & s_side

    pats = [[small_pattern(k, d) for k in small_ks] for d in range(2)]

    lbs = []
    for d in range(2):
        p = lb_ref[d]
        e = jnp.exp(p - jnp.max(p, axis=0, keepdims=True))
        sm = e / jnp.sum(e, axis=0, keepdims=True)
        lb = jnp.zeros((1, W), F32)
        for r in range(1, layer + 1):
            lb = lb + sm[r:r + 1, :]
        lbs.append(lb)
    views = ((qf_ref, vf_ref, zf_ref, of_ref), (qb_ref, vb_ref, zb_ref, ob_ref))

    def boundary_rows(b, c, d):
        off = c - 1 if d == 0 else c
        if 2 * c >= SUBLANES:
            parts = [jnp.broadcast_to(b[p0 + off:p0 + off + 1, :], (2 * c, B_F)) for p0 in range(0, L, 2 * c)]
        else:
            parts = [jnp.where(row8 < 4,
                               jnp.broadcast_to(b[g + off:g + off + 1, :], (SUBLANES, B_F)),
                               jnp.broadcast_to(b[g + 4 + off:g + 5 + off, :], (SUBLANES, B_F)))
                     for g in range(0, L, SUBLANES)]
        return parts[0] if len(parts) == 1 else jnp.concatenate(parts, axis=0)

    for idx in range(nch):
        for d in range(2):
            q_ref, v_ref, z_ref, o_ref = views[d]
            c0 = (idx if d == 0 else nch - 1 - idx) * L
            rows = slice(c0, c0 + L)
            for h in range(B_HEADS):
                chain = d * B_HEADS + h
                fs = slice(h * B_F, (h + 1) * B_F)
                vs = slice(h * B_V, (h + 1) * B_V)
                lb = lbs[d][:, fs]
                th = jnp.tanh(0.5 * z_ref[rows, fs])
                f = lb + (1.0 - lb) * (0.5 + 0.5 * th)
                kk = (1.0 - lb) * (0.5 - 0.5 * th)
                b2 = _scan_rows(jnp.log2(f), d == 1, jnp.add)
                bl2 = b2[L - 1:L, :] if d == 0 else b2[0:1, :]
                q_bf = q_ref[rows, fs]
                q = q_bf.astype(F32)
                v_bf = v_ref[rows, vs]
                t_upper = d == 0

                def level_product(k):
                    c = 1 << k
                    if c >= SUBLANES:
                        lo_src, hi_src = (kk, q) if t_upper else (q, kk)
                        parts = []
                        for p0 in range(0, L, 2 * c):
                            parts += [lo_src[p0:p0 + c], hi_src[p0 + c:p0 + 2 * c]]
                        x = jnp.concatenate(parts, axis=0)
                    else:
                        is_upper = (shr(row8, k) & 1) == 1
                        is_t = is_upper if t_upper else jnp.logical_not(is_upper)
                        qq = q * f if c == 1 else q
                        x = jnp.concatenate([jnp.where(is_t, qq[g:g + SUBLANES], kk[g:g + SUBLANES])
                                             for g in range(0, L, SUBLANES)], axis=0)
                    if c > 1:
                        x = x * jnp.exp2(-jnp.abs(b2 - boundary_rows(b2, c, d)))
                    w = x.astype(BF16)
                    return dot_nt(w, w)

                def row_group(x, j):
                    return x[j * SUBLANES:(j + 1) * SUBLANES]

                prod = dot_nt(q_bf, kk.astype(BF16))
                acc = [jnp.where(eye_pat, row_group(prod, j), 0.0) for j in range(n_sub)]
                for k in small_ks:
                    prod = level_product(k)
                    acc = [jnp.where(pats[d][k], row_group(prod, j), acc[j]) for j in range(n_sub)]
                acc = [jnp.where(lane_grp[k_sub] == j, acc[j], 0.0) for j in range(n_sub)]
                for k in big_ks:
                    prod = level_product(k)
                    for j in range(n_sub):
                        grp = (j * SUBLANES) >> k
                        if (grp & 1) == (1 if t_upper else 0):
                            sib = grp - 1 if t_upper else grp + 1
                            acc[j] = jnp.where(lane_grp[k] == sib, row_group(prod, j), acc[j])
                a = jnp.concatenate(acc, axis=0)
                st = st_ref[chain]
                qe = (q * jnp.exp2(b2)).astype(BF16)
                ke = (kk * jnp.exp2(bl2 - b2)).astype(BF16)
                o_ref[rows, vs] = (dot_nt(qe, st.astype(BF16)) + _dot(a.astype(BF16), v_bf)).astype(o_ref.dtype)
                st_ref[chain] = st * jnp.exp2(bl2) + lax.dot_general(
                    v_bf, ke, (((0,), (0,)), ((), ())), preferred_element_type=F32)


def _hgrn_mixer(p_main, zgates, b_lb, layer, batch, seq):
    m = batch * seq
    rows = min(MIX_ROWS, seq)
    nb = seq // rows
    fwd = lambda b, j: b * nb + j
    bwd = lambda b, j: b * nb + nb - 1 - j

    def specs(row, zcol):
        return [pl.BlockSpec((rows, 1024), lambda b, j: (row(b, j), 0)),
                pl.BlockSpec((rows, 1024), lambda b, j: (row(b, j), 1)),
                pl.BlockSpec((rows, 1024), lambda b, j: (row(b, j), zcol))]

    return pl.pallas_call(
        functools.partial(_hgrn_kernel, chunk=B_CHUNK, layer=layer),
        grid=(batch, nb),
        in_specs=specs(fwd, 0) + specs(bwd, 1) + [_const_spec(b_lb.shape)],
        out_specs=[pl.BlockSpec((rows, D_MODEL), lambda b, j: (fwd(b, j), 0)),
                   pl.BlockSpec((rows, D_MODEL), lambda b, j: (bwd(b, j), 0))],
        out_shape=[jax.ShapeDtypeStruct((m, D_MODEL), BF16)] * 2,
        scratch_shapes=[pltpu.VMEM((2 * B_HEADS, B_V, B_F), F32),
                        pltpu.VMEM((T_SLOTS, B_F, B_CHUNK), BF16)],
        compiler_params=_params(("parallel", "arbitrary")),
        name="hgrn_mixer",
    )(p_main, p_main, zgates, p_main, p_main, zgates, b_lb)


def _mix_out_kernel(hf_ref, hb_ref, gate_ref, xq_ref, kv_ref, wo_ref, gh_ref, gp_ref, x_ref, o_ref,
                    *, head_dim, silu_gate):
    h = hf_ref[...].astype(F32) + hb_ref[...].astype(F32)
    parts = []
    for c0 in range(0, D_MODEL, head_dim):
        hh = h[:, c0:c0 + head_dim]
        parts.append(hh * lax.rsqrt(jnp.mean(hh * hh, axis=-1, keepdims=True) + EPS))
    gate = gate_ref[...].astype(F32)
    gate = gate * _sigmoid(gate) if silu_gate else _sigmoid(gate)
    mix = (jnp.concatenate(parts, axis=-1) * gh_ref[...] * gate).astype(BF16)
    y = _dot(mix, wo_ref[0:D_MODEL, :])

    xa = []
    for hd in range(X_HEADS):
        cs = slice(hd * X_HD, (hd + 1) * X_HD)
        s = _dot_nt(xq_ref[:, cs], kv_ref[:, cs]) * (X_HD ** -0.5)
        e = jnp.exp(s - jnp.max(s, axis=-1, keepdims=True))
        pv = _dot(e.astype(BF16), kv_ref[:, X_W + hd * X_HD:X_W + (hd + 1) * X_HD])
        xa.append((pv / jnp.sum(e, axis=-1, keepdims=True)).astype(BF16))
    y = y + _dot(jnp.concatenate(xa, axis=-1), wo_ref[D_MODEL:D_MODEL + X_W, :])
    o_ref[...] = x_ref[...] + _rms(y, gp_ref[...])


def _mix_out(hf, hb, p_main, kv, w_out, g_head, g_post, x2d, batch, seq, head_dim, silu_gate):
    m = batch * seq
    tm = min(ROW_TILE, seq)
    nb = seq // tm
    row = lambda b, j: b * nb + j
    tile = pl.BlockSpec((tm, D_MODEL), lambda b, j: (row(b, j), 0))
    return pl.pallas_call(
        functools.partial(_mix_out_kernel, head_dim=head_dim, silu_gate=silu_gate),
        grid=(batch, nb),
        in_specs=[tile, tile,
                  pl.BlockSpec((tm, D_MODEL), lambda b, j: (row(b, j), GATE_COL_BLOCK)),
                  pl.BlockSpec((tm, X_W), lambda b, j: (row(b, j), XQ_COL_BLOCK)),
                  pl.BlockSpec((N_MEM, 2 * X_W), lambda b, j: (b, 0)),
                  _const_spec(w_out.shape), _const_spec((1, D_MODEL)), _const_spec((1, D_MODEL)),
                  tile],
        out_specs=tile,
        out_shape=jax.ShapeDtypeStruct((m, D_MODEL), F32),
        compiler_params=_params(("parallel", "parallel")),
        name="mix_out",
    )(hf, hb, p_main, p_main, kv, w_out, g_head.reshape(1, D_MODEL), g_post.reshape(1, D_MODEL), x2d)


HALO = 16


def _ffn_kernel(x_ref, xp_ref, xn_ref, gpre_ref, wup_ref, cw_ref, cb_ref, wdn_ref, gpost_ref, o_ref,
                hext_ref, act_ref, *, d_ff, fc, groups):
    tm = x_ref.shape[0]
    j = pl.program_id(1)
    gpre = gpre_ref[...]
    x = x_ref[...]
    has_prev = (j > 0).astype(F32)
    has_next = (j < pl.num_programs(1) - 1).astype(F32)
    hext_ref[0:HALO, :] = (_rms(xp_ref[...], gpre) * has_prev).astype(BF16)
    hext_ref[HALO:HALO + tm, :] = _rms(x, gpre).astype(BF16)
    hext_ref[HALO + tm:HALO + tm + HALO, :] = (_rms(xn_ref[...], gpre) * has_next).astype(BF16)

    starts = list(range(0, d_ff, fc))
    per_group = -(-len(starts) // groups)
    n_ext = tm + 2 * HALO
    f = None
    for ci, c0 in enumerate(starts):
        cw = min(fc, d_ff - c0)
        conv = []
        for base in (c0, d_ff + c0):
            cs = slice(base, base + cw)
            ue = _dot(hext_ref[...], wup_ref[:, cs])
            u_prev = pltpu.roll(ue, 1, 0)[HALO:HALO + tm]
            u_next = pltpu.roll(ue, n_ext - 1, 0)[HALO:HALO + tm]
            conv.append(u_prev * cw_ref[0:1, cs] + ue[HALO:HALO + tm] * cw_ref[1:2, cs]
                        + u_next * cw_ref[2:3, cs] + cb_ref[:, cs])
        a, g = conv
        act_ref[:, c0:c0 + cw] = (g * _sigmoid(g) * a).astype(BF16)
        if (ci + 1) % per_group == 0 or ci == len(starts) - 1:
            k0 = starts[(ci // per_group) * per_group]
            part_f = _dot(act_ref[:, k0:c0 + cw], wdn_ref[k0:c0 + cw, :])
            f = part_f if f is None else f + part_f
    o_ref[...] = x + _rms(f, gpost_ref[...])


def _ffn(x2d, g_pre, w_up, conv_w, conv_b, w_down, g_post, batch, seq):
    m = batch * seq
    tm = min(FFN_TILE, seq)
    nb = seq // tm
    hb = tm // HALO
    last_halo = m // HALO - 1
    d_ff = w_down.shape[0]
    tile = pl.BlockSpec((tm, D_MODEL), lambda b, j: (b * nb + j, 0))
    return pl.pallas_call(
        functools.partial(_ffn_kernel, d_ff=d_ff, fc=FF_CHUNK, groups=FF_GROUPS),
        grid=(batch, nb),
        in_specs=[tile,
                  pl.BlockSpec((HALO, D_MODEL), lambda b, j: (jnp.maximum((b * nb + j) * hb - 1, 0), 0)),
                  pl.BlockSpec((HALO, D_MODEL), lambda b, j: (jnp.minimum((b * nb + j + 1) * hb, last_halo), 0)),
                  _const_spec((1, D_MODEL)), _const_spec(w_up.shape), _const_spec(conv_w.shape),
                  _const_spec((1, 2 * d_ff)), _const_spec(w_down.shape), _const_spec((1, D_MODEL))],
        out_specs=tile,
        out_shape=jax.ShapeDtypeStruct((m, D_MODEL), F32),
        scratch_shapes=[pltpu.VMEM((tm + 2 * HALO, D_MODEL), BF16),
                        pltpu.VMEM((tm, d_ff), BF16)],
        compiler_params=_params(("parallel", "parallel")),
        name="ffn",
    )(x2d, x2d, x2d, g_pre.reshape(1, D_MODEL), w_up, conv_w, conv_b.reshape(1, 2 * d_ff), w_down,
      g_post.reshape(1, D_MODEL))


def _encoder_stack(x, mem, g_pre_mix, g_post_mix, g_mem, a_w_main, a_w_gate, a_bias, a_g_head, a_w_out,
                   b_w_main, b_w_gate, b_lb, b_g_head, b_w_out, x_w_kv, g_pre_ffn, g_post_ffn,
                   f_w_up, f_conv_w, f_conv_b, f_w_down):
    batch, seq, d = x.shape
    x2d = x.reshape(batch * seq, d)
    mem2d = mem.reshape(batch * N_MEM, d)
    depth = g_pre_mix.shape[0]
    for i in range(depth):
        j = i // 2
        kv = _norm_proj(mem2d, g_mem[i], x_w_kv[i])
        if i % 2 == 0:
            p_main, gates = _norm_proj(x2d, g_pre_mix[i], a_w_main[j], a_w_gate[j])
            hf, hb = _mlstm_mixer(p_main, gates, a_bias[j], batch, seq)
            x2d = _mix_out(hf, hb, p_main, kv, a_w_out[j], a_g_head[j], g_post_mix[i], x2d, batch, seq,
                           head_dim=A_V, silu_gate=False)
        else:
            p_main, zgates = _norm_proj(x2d, g_pre_mix[i], b_w_main[j], b_w_gate[j])
            hf, hb = _hgrn_mixer(p_main, zgates, b_lb, j, batch, seq)
            x2d = _mix_out(hf, hb, p_main, kv, b_w_out[j], b_g_head[j], g_post_mix[i], x2d, batch, seq,
                           head_dim=B_V, silu_gate=True)
        x2d = _ffn(x2d, g_pre_ffn[i], f_w_up[i], f_conv_w[i], f_conv_b[i], f_w_down[i], g_post_ffn[i],
                   batch, seq)
    return x2d.reshape(batch, seq, d)


def kernel(x_prompt, x_sample, mem_prompt, mem_sample, g_pre_mix, g_post_mix, g_mem, a_w_in, a_b_gate, a_g_head, a_w_out, b_w_in, b_lb, b_g_head, b_w_out, x_w_kv, g_pre_ffn, g_post_ffn, f_w_up, f_conv_w, f_conv_b, f_w_down):
    a_rec = 2 * A_HEADS * A_QK + 2 * A_HEADS * A_V
    n_gate = 4 * A_HEADS
    a_w_main = jnp.concatenate([a_w_in[:, :, :a_rec], a_w_in[:, :, a_rec + n_gate:]], axis=-1).astype(BF16)
    def two_tiles(x):
        h = A_HEADS
        zeros = jnp.zeros(x.shape[:-1] + (LANES - 2 * h,), x.dtype)
        i_f, f_f, i_b, f_b = (x[..., n * h:(n + 1) * h] for n in range(4))
        return jnp.concatenate([i_f, i_b, zeros, f_f, f_b, zeros], axis=-1)

    a_w_gate = two_tiles(a_w_in[:, :, a_rec:a_rec + n_gate]).astype(BF16)
    a_bias = two_tiles(a_b_gate).reshape(-1, 1, A_GATE_LANES)
    w = B_HEADS * B_F
    b_w_main = jnp.concatenate([b_w_in[:, :, :2 * w], b_w_in[:, :, 4 * w:]], axis=-1).astype(BF16)
    b_w_gate = b_w_in[:, :, 2 * w:4 * w].astype(BF16)
    shared = (g_pre_mix, g_post_mix, g_mem, a_w_main, a_w_gate, a_bias, a_g_head, a_w_out.astype(BF16),
              b_w_main, b_w_gate, b_lb, b_g_head, b_w_out.astype(BF16), x_w_kv.astype(BF16),
              g_pre_ffn, g_post_ffn, f_w_up.astype(BF16), f_conv_w, f_conv_b, f_w_down.astype(BF16))
    return (_encoder_stack(x_prompt, mem_prompt, *shared), _encoder_stack(x_sample, mem_sample, *shared))
```

```python
import functools

import jax
import jax.numpy as jnp
from jax import lax
from jax.experimental import pallas as pl
from jax.experimental.pallas import tpu as pltpu

F32 = jnp.float32
BF16 = jnp.bfloat16
EPS = 1e-6

D_MODEL = 1024
N_MEM = 256
X_HEADS = 4
X_HD = 128
X_W = X_HEADS * X_HD
A_HEADS = 4
A_QK = 128
A_V = 256
B_HEADS = 8
B_F = 128
B_V = 128

GATE_COL_BLOCK = 2
XQ_COL_BLOCK = 6
LANES = 128
SUBLANES = 8
A_GATE_LANES = 2 * LANES
A_VA = A_V + LANES

VMEM_LIMIT_BYTES = 56 * 1024 * 1024

ROW_TILE = 512
FFN_TILE = 1024
MIX_ROWS = 512
A_CHUNK = 128
B_CHUNK = 128
FF_CHUNK = 256
FF_GROUPS = 2
T_SLOTS = 8


def _params(sem):
    return pltpu.CompilerParams(dimension_semantics=sem, vmem_limit_bytes=VMEM_LIMIT_BYTES)


def _const_spec(shape):
    nd = len(shape)
    return pl.BlockSpec(shape, lambda *_: (0,) * nd, pipeline_mode=pl.Buffered(1))


def _rms(x, g):
    ms = jnp.mean(x * x, axis=-1, keepdims=True)
    return x * lax.rsqrt(ms + EPS) * g


def _sigmoid(x):
    return 1.0 / (1.0 + jnp.exp(-x))


def _log_sigmoid(x):
    return jnp.minimum(x, 0.0) - jnp.log(1.0 + jnp.exp(-jnp.abs(x)))


def _dot(a, b):
    return jnp.dot(a, b, preferred_element_type=F32)


def _dot_nt(a, b):
    return lax.dot_general(a, b, (((1,), (1,)), ((), ())), preferred_element_type=F32)


def _cumsum_rows(tri_bf, x):
    hi = x.astype(BF16)
    r1 = x - hi.astype(F32)
    mid = r1.astype(BF16)
    lo = (r1 - mid.astype(F32)).astype(BF16)
    return _dot(tri_bf, hi) + _dot(tri_bf, mid) + _dot(tri_bf, lo)


def _scan_rows(x, reverse, op):
    n = x.shape[0] // SUBLANES
    row8 = lax.broadcasted_iota(jnp.int32, (SUBLANES, 1), 0)
    ident = 0.0 if op is jnp.add else -jnp.inf
    parts = [x[i * SUBLANES:(i + 1) * SUBLANES] for i in range(n)]
    for s in (1, 2, 4):
        for i in range(n):
            p = parts[i]
            if not reverse:
                parts[i] = op(p, jnp.where(row8 >= s, pltpu.roll(p, s, 0), ident))
            else:
                parts[i] = op(p, jnp.where(row8 < SUBLANES - s, pltpu.roll(p, SUBLANES - s, 0), ident))
    carry = None
    for i in (range(n) if not reverse else range(n - 1, -1, -1)):
        if carry is not None:
            parts[i] = op(parts[i], carry)
        edge = parts[i][SUBLANES - 1:SUBLANES] if not reverse else parts[i][0:1]
        carry = jnp.broadcast_to(edge, (SUBLANES, x.shape[1]))
    return jnp.concatenate(parts, axis=0)


def _lane_bcast(x, c):
    return jnp.broadcast_to(x[:, c:c + 1], (x.shape[0], LANES))


def _norm_proj_kernel(x_ref, g_ref, wa_ref, *rest, col_chunk):
    if len(rest) == 3:
        wb_ref, oa_ref, ob_ref = rest
    else:
        (oa_ref,) = rest
        wb_ref = ob_ref = None
    h = _rms(x_ref[...], g_ref[...]).astype(BF16)
    for w_ref, o_ref in ((wa_ref, oa_ref), (wb_ref, ob_ref)):
        if w_ref is None:
            continue
        n = w_ref.shape[1]
        for c0 in range(0, n, col_chunk):
            cw = min(col_chunk, n - c0)
            o_ref[:, c0:c0 + cw] = _dot(h, w_ref[:, c0:c0 + cw]).astype(o_ref.dtype)


def _norm_proj(x2d, g, wa, wb=None):
    m, d = x2d.shape
    tm = min(ROW_TILE, m)
    in_specs = [pl.BlockSpec((tm, d), lambda i: (i, 0)), _const_spec((1, d)), _const_spec(wa.shape)]
    out_shape = [jax.ShapeDtypeStruct((m, wa.shape[1]), BF16)]
    out_specs = [pl.BlockSpec((tm, wa.shape[1]), lambda i: (i, 0))]
    args = [x2d, g.reshape(1, d), wa]
    if wb is not None:
        in_specs.append(_const_spec(wb.shape))
        out_shape.append(jax.ShapeDtypeStruct((m, wb.shape[1]), F32))
        out_specs.append(pl.BlockSpec((tm, wb.shape[1]), lambda i: (i, 0)))
        args.append(wb)
    out = pl.pallas_call(
        functools.partial(_norm_proj_kernel, col_chunk=512),
        grid=(m // tm,),
        in_specs=in_specs,
        out_specs=out_specs,
        out_shape=out_shape,
        compiler_params=_params(("parallel",)),
        name="norm_proj",
    )(*args)
    return out if wb is not None else out[0]


def _mlstm_kernel(qf_ref, kf_ref, vf_ref, gf_ref, qb_ref, kb_ref, vb_ref, gb_ref, bias_ref,
                  hf_ref, hb_ref, c_ref, m_ref, *, chunk):
    L = chunk
    nch = qf_ref.shape[0] // L
    scale = A_QK ** -0.5

    @pl.when(pl.program_id(1) == 0)
    def _():
        c_ref[...] = jnp.zeros_like(c_ref)
        m_ref[...] = jnp.zeros_like(m_ref)

    ti = lax.broadcasted_iota(jnp.int32, (L, L), 0)
    si = lax.broadcasted_iota(jnp.int32, (L, L), 1)
    masks = (si <= ti, si >= ti)
    tris = (masks[0].astype(BF16), masks[1].astype(BF16))
    bias_i = bias_ref[:, 0:LANES]
    bias_f = bias_ref[:, LANES:2 * LANES]
    ones = jnp.ones((L, LANES), BF16)
    zeros = jnp.zeros((L, LANES), BF16)
    views = ((qf_ref, kf_ref, vf_ref, gf_ref, hf_ref), (qb_ref, kb_ref, vb_ref, gb_ref, hb_ref))

    for idx in range(nch):
        for d in range(2):
            q_ref, k_ref, v_ref, g_ref, h_ref = views[d]
            c0 = (idx if d == 0 else nch - 1 - idx) * L
            rows = slice(c0, c0 + L)
            li = g_ref[rows, 0:LANES] + bias_i
            cum = _cumsum_rows(tris[d], _log_sigmoid(g_ref[rows, LANES:2 * LANES] + bias_f))
            u = li - cum
            cm = _scan_rows(u, d == 1, jnp.maximum)
            u_t = u.T
            for hd in range(A_HEADS):
                chain = d * A_HEADS + hd
                b_bc = _lane_bcast(cum, chain)
                cm_bc = _lane_bcast(cm, chain)
                u_row = u_t[chain:chain + 1, :]
                bl = b_bc[L - 1:L, :] if d == 0 else b_bc[0:1, :]
                q = q_ref[rows, hd * A_QK:(hd + 1) * A_QK]
                k = k_ref[rows, hd * A_QK:(hd + 1) * A_QK]
                v = v_ref[rows, hd * A_V:(hd + 1) * A_V]
                v_aug = jnp.concatenate([v, ones], axis=1)
                v_zero = jnp.concatenate([v, zeros], axis=1)
                s = _dot_nt(q, k) * jnp.where(masks[d], jnp.exp(u_row - cm_bc), 0.0)
                g_row = u_row + bl
                g_max = jnp.max(g_row, axis=1, keepdims=True)
                kw_t = (k.T.astype(F32) * jnp.exp(g_row - g_max)).astype(BF16)
                dca = _dot(kw_t, v_aug)
                m_old = m_ref[chain]
                c_old = c_ref[chain]
                mx = jnp.maximum(m_old, cm_bc)
                a_loc = scale * jnp.exp(cm_bc - mx)
                a_int = scale * jnp.exp(m_old - mx)
                lhs = jnp.concatenate([(a_loc * s).astype(BF16), (a_int * q.astype(F32)).astype(BF16)], axis=1)
                nd = _dot(lhs, jnp.concatenate([v_zero, c_old.astype(BF16)], axis=0))
                den = a_loc * jnp.sum(s, axis=1, keepdims=True) + nd[:, A_V:]
                rden = 1.0 / jnp.maximum(jnp.abs(den), jnp.exp(-(b_bc + mx)))
                for c1 in range(0, A_V, LANES):
                    h_ref[rows, hd * A_V + c1:hd * A_V + c1 + LANES] = (nd[:, c1:c1 + LANES] * rden).astype(h_ref.dtype)
                g_max_b = jnp.broadcast_to(g_max, (1, LANES))
                m_new = jnp.maximum(bl + m_old, g_max_b)
                dec = jnp.exp(bl + m_old - m_new)
                inc = jnp.exp(g_max_b - m_new)
                n_tiles = A_VA // LANES
                c_ref[chain] = (jnp.concatenate([dec] * n_tiles, axis=1) * c_old
                                + jnp.concatenate([inc] * n_tiles, axis=1) * dca)
                m_ref[chain] = m_new


def _mlstm_mixer(p_main, gates, bias, batch, seq):
    m = batch * seq
    rows = min(MIX_ROWS, seq)
    nb = seq // rows
    fwd = lambda b, j: b * nb + j
    bwd = lambda b, j: b * nb + nb - 1 - j

    def specs(row):
        return [pl.BlockSpec((rows, 512), lambda b, j: (row(b, j), 0)),
                pl.BlockSpec((rows, 512), lambda b, j: (row(b, j), 1)),
                pl.BlockSpec((rows, 1024), lambda b, j: (row(b, j), 1)),
                pl.BlockSpec((rows, A_GATE_LANES), lambda b, j: (row(b, j), 0))]

    n_chain = 2 * A_HEADS
    return pl.pallas_call(
        functools.partial(_mlstm_kernel, chunk=A_CHUNK),
        grid=(batch, nb),
        in_specs=specs(fwd) + specs(bwd) + [_const_spec((1, A_GATE_LANES))],
        out_specs=[pl.BlockSpec((rows, D_MODEL), lambda b, j: (fwd(b, j), 0)),
                   pl.BlockSpec((rows, D_MODEL), lambda b, j: (bwd(b, j), 0))],
        out_shape=[jax.ShapeDtypeStruct((m, D_MODEL), BF16)] * 2,
        scratch_shapes=[pltpu.VMEM((n_chain, A_QK, A_VA), F32),
                        pltpu.VMEM((n_chain, 1, LANES), F32)],
        compiler_params=_params(("parallel", "arbitrary")),
        name="mlstm_mixer",
    )(p_main, p_main, p_main, gates, p_main, p_main, p_main, gates, bias)


def _hgrn_kernel(qf_ref, vf_ref, zf_ref, qb_ref, vb_ref, zb_ref, lb_ref, of_ref, ob_ref, st_ref, t_scr,
                 *, chunk, layer):
    L = chunk
    nch = qf_ref.shape[0] // L
    W = B_HEADS * B_F

    @pl.when(pl.program_id(1) == 0)
    def _():
        st_ref[...] = jnp.zeros_like(st_ref)

    n_slots = t_scr.shape[0]
    uses = [0]

    def dot_nt(a, b):
        slot = t_scr.at[uses[0] % n_slots]
        uses[0] += 1
        slot[...] = b.T
        return _dot(a, slot[...])

    row8 = lax.broadcasted_iota(jnp.int32, (SUBLANES, 1), 0)
    n_sub = L // SUBLANES
    k_sub = SUBLANES.bit_length() - 1
    small_ks = list(range(k_sub))
    big_ks = list(range(k_sub, L.bit_length() - 1))

    shr = lax.shift_right_logical
    row_i = lax.broadcasted_iota(jnp.int32, (SUBLANES, L), 0)
    lane_i = lax.broadcasted_iota(jnp.int32, (SUBLANES, L), 1)
    lane_in = lane_i & (SUBLANES - 1)
    eye_pat = row_i == lane_in
    lane_grp = {k: shr(lane_i, k) for k in big_ks}

    def small_pattern(k, d):
        same_pair = shr(row_i, k + 1) == shr(lane_in, k + 1)
        t_side = (shr(row_i, k) & 1) == (1 if d == 0 else 0)
        s_side = (shr(lane_in, k) & 1) == (0 if d == 0 else 1)
        return same_pair & t_side & s_side

    pats = [[small_pattern(k, d) for k in small_ks] for d in range(2)]

    lbs = []
    for d in range(2):
        p = lb_ref[d]
        e = jnp.exp(p - jnp.max(p, axis=0, keepdims=True))
        sm = e / jnp.sum(e, axis=0, keepdims=True)
        lb = jnp.zeros((1, W), F32)
        for r in range(1, layer + 1):
            lb = lb + sm[r:r + 1, :]
        lbs.append(lb)
    views = ((qf_ref, vf_ref, zf_ref, of_ref), (qb_ref, vb_ref, zb_ref, ob_ref))

    def boundary_rows(b, c, d):
        off = c - 1 if d == 0 else c
        if 2 * c >= SUBLANES:
            parts = [jnp.broadcast_to(b[p0 + off:p0 + off + 1, :], (2 * c, B_F)) for p0 in range(0, L, 2 * c)]
        else:
            parts = [jnp.where(row8 < 4,
                               jnp.broadcast_to(b[g + off:g + off + 1, :], (SUBLANES, B_F)),
                               jnp.broadcast_to(b[g + 4 + off:g + 5 + off, :], (SUBLANES, B_F)))
                     for g in range(0, L, SUBLANES)]
        return parts[0] if len(parts) == 1 else jnp.concatenate(parts, axis=0)

    for idx in range(nch):
        for d in range(2):
            q_ref, v_ref, z_ref, o_ref = views[d]
            c0 = (idx if d == 0 else nch - 1 - idx) * L
            rows = slice(c0, c0 + L)
            for h in range(B_HEADS):
                chain = d * B_HEADS + h
                fs = slice(h * B_F, (h + 1) * B_F)
                vs = slice(h * B_V, (h + 1) * B_V)
                lb = lbs[d][:, fs]
                z = z_ref[rows, fs]
                f = lb + (1.0 - lb) * _sigmoid(z)
                kk = (1.0 - lb) * _sigmoid(-z)
                b2 = _scan_rows(jnp.log2(f), d == 1, jnp.add)
                bl2 = b2[L - 1:L, :] if d == 0 else b2[0:1, :]
                q_bf = q_ref[rows, fs]
                q = q_bf.astype(F32)
                v_bf = v_ref[rows, vs]
                t_upper = d == 0

                def level_product(k):
                    c = 1 << k
                    if c >= SUBLANES:
                        lo_src, hi_src = (kk, q) if t_upper else (q, kk)
                        parts = []
                        for p0 in range(0, L, 2 * c):
                            parts += [lo_src[p0:p0 + c], hi_src[p0 + c:p0 + 2 * c]]
                        x = jnp.concatenate(parts, axis=0)
                    else:
                        is_upper = (shr(row8, k) & 1) == 1
                        is_t = is_upper if t_upper else jnp.logical_not(is_upper)
                        qq = q * f if c == 1 else q
                        x = jnp.concatenate([jnp.where(is_t, qq[g:g + SUBLANES], kk[g:g + SUBLANES])
                                             for g in range(0, L, SUBLANES)], axis=0)
                    if c > 1:
                        x = x * jnp.exp2(-jnp.abs(b2 - boundary_rows(b2, c, d)))
                    w = x.astype(BF16)
                    return dot_nt(w, w)

                def row_group(x, j):
                    return x[j * SUBLANES:(j + 1) * SUBLANES]

                prod = dot_nt(q_bf, kk.astype(BF16))
                acc = [jnp.where(eye_pat, row_group(prod, j), 0.0) for j in range(n_sub)]
                for k in small_ks:
                    prod = level_product(k)
                    acc = [jnp.where(pats[d][k], row_group(prod, j), acc[j]) for j in range(n_sub)]
                acc = [jnp.where(lane_grp[k_sub] == j, acc[j], 0.0) for j in range(n_sub)]
                for k in big_ks:
                    prod = level_product(k)
                    for j in range(n_sub):
                        grp = (j * SUBLANES) >> k
                        if (grp & 1) == (1 if t_upper else 0):
                            sib = grp - 1 if t_upper else grp + 1
                            acc[j] = jnp.where(lane_grp[k] == sib, row_group(prod, j), acc[j])
                a = jnp.concatenate(acc, axis=0)
                st = st_ref[chain]
                qe = (q * jnp.exp2(b2)).astype(BF16)
                ke = (kk * jnp.exp2(bl2 - b2)).astype(BF16)
                o_ref[rows, vs] = (dot_nt(qe, st.astype(BF16)) + _dot(a.astype(BF16), v_bf)).astype(o_ref.dtype)
                st_ref[chain] = st * jnp.exp2(bl2) + lax.dot_general(
                    v_bf, ke, (((0,), (0,)), ((), ())), preferred_element_type=F32)


def _hgrn_mixer(p_main, zgates, b_lb, layer, batch, seq):
    m = batch * seq
    rows = min(MIX_ROWS, seq)
    nb = seq // rows
    fwd = lambda b, j: b * nb + j
    bwd = lambda b, j: b * nb + nb - 1 - j

    def specs(row, zcol):
        return [pl.BlockSpec((rows, 1024), lambda b, j: (row(b, j), 0)),
                pl.BlockSpec((rows, 1024), lambda b, j: (row(b, j), 1)),
                pl.BlockSpec((rows, 1024), lambda b, j: (row(b, j), zcol))]

    return pl.pallas_call(
        functools.partial(_hgrn_kernel, chunk=B_CHUNK, layer=layer),
        grid=(batch, nb),
        in_specs=specs(fwd, 0) + specs(bwd, 1) + [_const_spec(b_lb.shape)],
        out_specs=[pl.BlockSpec((rows, D_MODEL), lambda b, j: (fwd(b, j), 0)),
                   pl.BlockSpec((rows, D_MODEL), lambda b, j: (bwd(b, j), 0))],
        out_shape=[jax.ShapeDtypeStruct((m, D_MODEL), BF16)] * 2,
        scratch_shapes=[pltpu.VMEM((2 * B_HEADS, B_V, B_F), F32),
                        pltpu.VMEM((T_SLOTS, B_F, B_CHUNK), BF16)],
        compiler_params=_params(("parallel", "arbitrary")),
        name="hgrn_mixer",
    )(p_main, p_main, zgates, p_main, p_main, zgates, b_lb)


def _mix_out_kernel(hf_ref, hb_ref, gate_ref, xq_ref, kv_ref, wo_ref, gh_ref, gp_ref, x_ref, o_ref,
                    *, head_dim, silu_gate):
    h = hf_ref[...].astype(F32) + hb_ref[...].astype(F32)
    parts = []
    for c0 in range(0, D_MODEL, head_dim):
        hh = h[:, c0:c0 + head_dim]
        parts.append(hh * lax.rsqrt(jnp.mean(hh * hh, axis=-1, keepdims=True) + EPS))
    gate = gate_ref[...].astype(F32)
    gate = gate * _sigmoid(gate) if silu_gate else _sigmoid(gate)
    mix = (jnp.concatenate(parts, axis=-1) * gh_ref[...] * gate).astype(BF16)
    y = _dot(mix, wo_ref[0:D_MODEL, :])

    xa = []
    for hd in range(X_HEADS):
        cs = slice(hd * X_HD, (hd + 1) * X_HD)
        s = _dot_nt(xq_ref[:, cs], kv_ref[:, cs]) * (X_HD ** -0.5)
        e = jnp.exp(s - jnp.max(s, axis=-1, keepdims=True))
        pv = _dot(e.astype(BF16), kv_ref[:, X_W + hd * X_HD:X_W + (hd + 1) * X_HD])
        xa.append((pv / jnp.sum(e, axis=-1, keepdims=True)).astype(BF16))
    y = y + _dot(jnp.concatenate(xa, axis=-1), wo_ref[D_MODEL:D_MODEL + X_W, :])
    o_ref[...] = x_ref[...] + _rms(y, gp_ref[...])


def _mix_out(hf, hb, p_main, kv, w_out, g_head, g_post, x2d, batch, seq, head_dim, silu_gate):
    m = batch * seq
    tm = min(ROW_TILE, seq)
    nb = seq // tm
    row = lambda b, j: b * nb + j
    tile = pl.BlockSpec((tm, D_MODEL), lambda b, j: (row(b, j), 0))
    return pl.pallas_call(
        functools.partial(_mix_out_kernel, head_dim=head_dim, silu_gate=silu_gate),
        grid=(batch, nb),
        in_specs=[tile, tile,
                  pl.BlockSpec((tm, D_MODEL), lambda b, j: (row(b, j), GATE_COL_BLOCK)),
                  pl.BlockSpec((tm, X_W), lambda b, j: (row(b, j), XQ_COL_BLOCK)),
                  pl.BlockSpec((N_MEM, 2 * X_W), lambda b, j: (b, 0)),
                  _const_spec(w_out.shape), _const_spec((1, D_MODEL)), _const_spec((1, D_MODEL)),
                  tile],
        out_specs=tile,
        out_shape=jax.ShapeDtypeStruct((m, D_MODEL), F32),
        compiler_params=_params(("parallel", "parallel")),
        name="mix_out",
    )(hf, hb, p_main, p_main, kv, w_out, g_head.reshape(1, D_MODEL), g_post.reshape(1, D_MODEL), x2d)


HALO = 16


def _ffn_kernel(x_ref, xp_ref, xn_ref, gpre_ref, wup_ref, cw_ref, cb_ref, wdn_ref, gpost_ref, o_ref,
                hext_ref, act_ref, *, d_ff, fc, groups):
    tm = x_ref.shape[0]
    j = pl.program_id(1)
    gpre = gpre_ref[...]
    x = x_ref[...]
    has_prev = (j > 0).astype(F32)
    has_next = (j < pl.num_programs(1) - 1).astype(F32)
    hext_ref[0:HALO, :] = (_rms(xp_ref[...], gpre) * has_prev).astype(BF16)
    hext_ref[HALO:HALO + tm, :] = _rms(x, gpre).astype(BF16)
    hext_ref[HALO + tm:HALO + tm + HALO, :] = (_rms(xn_ref[...], gpre) * has_next).astype(BF16)

    starts = list(range(0, d_ff, fc))
    per_group = -(-len(starts) // groups)
    n_ext = tm + 2 * HALO
    f = None
    for ci, c0 in enumerate(starts):
        cw = min(fc, d_ff - c0)
        conv = []
        for base in (c0, d_ff + c0):
            cs = slice(base, base + cw)
            ue = _dot(hext_ref[...], wup_ref[:, cs])
            u_prev = pltpu.roll(ue, 1, 0)[HALO:HALO + tm]
            u_next = pltpu.roll(ue, n_ext - 1, 0)[HALO:HALO + tm]
            conv.append(u_prev * cw_ref[0:1, cs] + ue[HALO:HALO + tm] * cw_ref[1:2, cs]
                        + u_next * cw_ref[2:3, cs] + cb_ref[:, cs])
        a, g = conv
        act_ref[:, c0:c0 + cw] = (g * _sigmoid(g) * a).astype(BF16)
        if (ci + 1) % per_group == 0 or ci == len(starts) - 1:
            k0 = starts[(ci // per_group) * per_group]
            part_f = _dot(act_ref[:, k0:c0 + cw], wdn_ref[k0:c0 + cw, :])
            f = part_f if f is None else f + part_f
    o_ref[...] = x + _rms(f, gpost_ref[...])


def _ffn(x2d, g_pre, w_up, conv_w, conv_b, w_down, g_post, batch, seq):
    m = batch * seq
    tm = min(FFN_TILE, seq)
    nb = seq // tm
    hb = tm // HALO
    last_halo = m // HALO - 1
    d_ff = w_down.shape[0]
    tile = pl.BlockSpec((tm, D_MODEL), lambda b, j: (b * nb + j, 0))
    return pl.pallas_call(
        functools.partial(_ffn_kernel, d_ff=d_ff, fc=FF_CHUNK, groups=FF_GROUPS),
        grid=(batch, nb),
        in_specs=[tile,
                  pl.BlockSpec((HALO, D_MODEL), lambda b, j: (jnp.maximum((b * nb + j) * hb - 1, 0), 0)),
                  pl.BlockSpec((HALO, D_MODEL), lambda b, j: (jnp.minimum((b * nb + j + 1) * hb, last_halo), 0)),
                  _const_spec((1, D_MODEL)), _const_spec(w_up.shape), _const_spec(conv_w.shape),
                  _const_spec((1, 2 * d_ff)), _const_spec(w_down.shape), _const_spec((1, D_MODEL))],
        out_specs=tile,
        out_shape=jax.ShapeDtypeStruct((m, D_MODEL), F32),
        scratch_shapes=[pltpu.VMEM((tm + 2 * HALO, D_MODEL), BF16),
                        pltpu.VMEM((tm, d_ff), BF16)],
        compiler_params=_params(("parallel", "parallel")),
        name="ffn",
    )(x2d, x2d, x2d, g_pre.reshape(1, D_MODEL), w_up, conv_w, conv_b.reshape(1, 2 * d_ff), w_down,
      g_post.reshape(1, D_MODEL))


def _encoder_stack(x, mem, g_pre_mix, g_post_mix, g_mem, a_w_main, a_w_gate, a_bias, a_g_head, a_w_out,
                   b_w_main, b_w_gate, b_lb, b_g_head, b_w_out, x_w_kv, g_pre_ffn, g_post_ffn,
                   f_w_up, f_conv_w, f_conv_b, f_w_down):
    batch, seq, d = x.shape
    x2d = x.reshape(batch * seq, d)
    mem2d = mem.reshape(batch * N_MEM, d)
    depth = g_pre_mix.shape[0]
    for i in range(depth):
        j = i // 2
        kv = _norm_proj(mem2d, g_mem[i], x_w_kv[i])
        if i % 2 == 0:
            p_main, gates = _norm_proj(x2d, g_pre_mix[i], a_w_main[j], a_w_gate[j])
            hf, hb = _mlstm_mixer(p_main, gates, a_bias[j], batch, seq)
            x2d = _mix_out(hf, hb, p_main, kv, a_w_out[j], a_g_head[j], g_post_mix[i], x2d, batch, seq,
                           head_dim=A_V, silu_gate=False)
        else:
            p_main, zgates = _norm_proj(x2d, g_pre_mix[i], b_w_main[j], b_w_gate[j])
            hf, hb = _hgrn_mixer(p_main, zgates, b_lb, j, batch, seq)
            x2d = _mix_out(hf, hb, p_main, kv, b_w_out[j], b_g_head[j], g_post_mix[i], x2d, batch, seq,
                           head_dim=B_V, silu_gate=True)
        x2d = _ffn(x2d, g_pre_ffn[i], f_w_up[i], f_conv_w[i], f_conv_b[i], f_w_down[i], g_post_ffn[i],
                   batch, seq)
    return x2d.reshape(batch, seq, d)


def kernel(x_prompt, x_sample, mem_prompt, mem_sample, g_pre_mix, g_post_mix, g_mem, a_w_in, a_b_gate, a_g_head, a_w_out, b_w_in, b_lb, b_g_head, b_w_out, x_w_kv, g_pre_ffn, g_post_ffn, f_w_up, f_conv_w, f_conv_b, f_w_down):
    a_rec = 2 * A_HEADS * A_QK + 2 * A_HEADS * A_V
    n_gate = 4 * A_HEADS
    a_w_main = jnp.concatenate([a_w_in[:, :, :a_rec], a_w_in[:, :, a_rec + n_gate:]], axis=-1).astype(BF16)
    def two_tiles(x):
        h = A_HEADS
        zeros = jnp.zeros(x.shape[:-1] + (LANES - 2 * h,), x.dtype)
        i_f, f_f, i_b, f_b = (x[..., n * h:(n + 1) * h] for n in range(4))
        return jnp.concatenate([i_f, i_b, zeros, f_f, f_b, zeros], axis=-1)

    a_w_gate = two_tiles(a_w_in[:, :, a_rec:a_rec + n_gate]).astype(BF16)
    a_bias = two_tiles(a_b_gate).reshape(-1, 1, A_GATE_LANES)
    w = B_HEADS * B_F
    b_w_main = jnp.concatenate([b_w_in[:, :, :2 * w], b_w_in[:, :, 4 * w:]], axis=-1).astype(BF16)
    b_w_gate = b_w_in[:, :, 2 * w:4 * w].astype(BF16)
    shared = (g_pre_mix, g_post_mix, g_mem, a_w_main, a_w_gate, a_bias, a_g_head, a_w_out.astype(BF16),
              b_w_main, b_w_gate, b_lb, b_g_head, b_w_out.astype(BF16), x_w_kv.astype(BF16),
              g_pre_ffn, g_post_ffn, f_w_up.astype(BF16), f_conv_w, f_conv_b, f_w_down.astype(BF16))
    return (_encoder_stack(x_prompt, mem_prompt, *shared), _encoder_stack(x_sample, mem_sample, *shared))
```

```python
import functools

import jax
import jax.numpy as jnp
from jax import lax
from jax.experimental import pallas as pl
from jax.experimental.pallas import tpu as pltpu

F32 = jnp.float32
BF16 = jnp.bfloat16
EPS = 1e-6

D_MODEL = 1024
N_MEM = 256
X_HEADS = 4
X_HD = 128
X_W = X_HEADS * X_HD
A_HEADS = 4
A_QK = 128
A_V = 256
B_HEADS = 8
B_F = 128
B_V = 128

GATE_COL_BLOCK = 2
XQ_COL_BLOCK = 6
LANES = 128
SUBLANES = 8
A_GATE_LANES = 2 * LANES
A_VA = A_V + LANES

VMEM_LIMIT_BYTES = 56 * 1024 * 1024

ROW_TILE = 512
FFN_TILE = 1024
MIX_ROWS = 512
A_CHUNK = 128
B_CHUNK = 128
FF_CHUNK = 256
FF_GROUPS = 1
T_SLOTS = 8


def _params(sem):
    return pltpu.CompilerParams(dimension_semantics=sem, vmem_limit_bytes=VMEM_LIMIT_BYTES)


def _const_spec(shape):
    nd = len(shape)
    return pl.BlockSpec(shape, lambda *_: (0,) * nd, pipeline_mode=pl.Buffered(1))


def _rms(x, g):
    ms = jnp.mean(x * x, axis=-1, keepdims=True)
    return x * lax.rsqrt(ms + EPS) * g


def _sigmoid(x):
    return 1.0 / (1.0 + jnp.exp(-x))


def _log_sigmoid(x):
    return jnp.minimum(x, 0.0) - jnp.log(1.0 + jnp.exp(-jnp.abs(x)))


def _dot(a, b):
    return jnp.dot(a, b, preferred_element_type=F32)


def _dot_nt(a, b):
    return lax.dot_general(a, b, (((1,), (1,)), ((), ())), preferred_element_type=F32)


def _cumsum_rows(tri_bf, x):
    hi = x.astype(BF16)
    r1 = x - hi.astype(F32)
    mid = r1.astype(BF16)
    lo = (r1 - mid.astype(F32)).astype(BF16)
    return _dot(tri_bf, hi) + _dot(tri_bf, mid) + _dot(tri_bf, lo)


def _scan_rows(x, reverse, op):
    n = x.shape[0] // SUBLANES
    row8 = lax.broadcasted_iota(jnp.int32, (SUBLANES, 1), 0)
    ident = 0.0 if op is jnp.add else -jnp.inf
    parts = [x[i * SUBLANES:(i + 1) * SUBLANES] for i in range(n)]
    for s in (1, 2, 4):
        for i in range(n):
            p = parts[i]
            if not reverse:
                parts[i] = op(p, jnp.where(row8 >= s, pltpu.roll(p, s, 0), ident))
            else:
                parts[i] = op(p, jnp.where(row8 < SUBLANES - s, pltpu.roll(p, SUBLANES - s, 0), ident))
    carry = None
    for i in (range(n) if not reverse else range(n - 1, -1, -1)):
        if carry is not None:
            parts[i] = op(parts[i], carry)
        edge = parts[i][SUBLANES - 1:SUBLANES] if not reverse else parts[i][0:1]
        carry = jnp.broadcast_to(edge, (SUBLANES, x.shape[1]))
    return jnp.concatenate(parts, axis=0)


def _lane_bcast(x, c):
    return jnp.broadcast_to(x[:, c:c + 1], (x.shape[0], LANES))


def _norm_proj_kernel(x_ref, g_ref, wa_ref, *rest, col_chunk):
    if len(rest) == 3:
        wb_ref, oa_ref, ob_ref = rest
    else:
        (oa_ref,) = rest
        wb_ref = ob_ref = None
    h = _rms(x_ref[...], g_ref[...]).astype(BF16)
    for w_ref, o_ref in ((wa_ref, oa_ref), (wb_ref, ob_ref)):
        if w_ref is None:
            continue
        n = w_ref.shape[1]
        for c0 in range(0, n, col_chunk):
            cw = min(col_chunk, n - c0)
            o_ref[:, c0:c0 + cw] = _dot(h, w_ref[:, c0:c0 + cw]).astype(o_ref.dtype)


def _norm_proj(x2d, g, wa, wb=None):
    m, d = x2d.shape
    tm = min(ROW_TILE, m)
    in_specs = [pl.BlockSpec((tm, d), lambda i: (i, 0)), _const_spec((1, d)), _const_spec(wa.shape)]
    out_shape = [jax.ShapeDtypeStruct((m, wa.shape[1]), BF16)]
    out_specs = [pl.BlockSpec((tm, wa.shape[1]), lambda i: (i, 0))]
    args = [x2d, g.reshape(1, d), wa]
    if wb is not None:
        in_specs.append(_const_spec(wb.shape))
        out_shape.append(jax.ShapeDtypeStruct((m, wb.shape[1]), F32))
        out_specs.append(pl.BlockSpec((tm, wb.shape[1]), lambda i: (i, 0)))
        args.append(wb)
    out = pl.pallas_call(
        functools.partial(_norm_proj_kernel, col_chunk=512),
        grid=(m // tm,),
        in_specs=in_specs,
        out_specs=out_specs,
        out_shape=out_shape,
        compiler_params=_params(("parallel",)),
        name="norm_proj",
    )(*args)
    return out if wb is not None else out[0]


def _mlstm_kernel(qf_ref, kf_ref, vf_ref, gf_ref, qb_ref, kb_ref, vb_ref, gb_ref, bias_ref,
                  hf_ref, hb_ref, c_ref, m_ref, *, chunk):
    L = chunk
    nch = qf_ref.shape[0] // L
    scale = A_QK ** -0.5

    @pl.when(pl.program_id(1) == 0)
    def _():
        c_ref[...] = jnp.zeros_like(c_ref)
        m_ref[...] = jnp.zeros_like(m_ref)

    ti = lax.broadcasted_iota(jnp.int32, (L, L), 0)
    si = lax.broadcasted_iota(jnp.int32, (L, L), 1)
    masks = (si <= ti, si >= ti)
    tris = (masks[0].astype(BF16), masks[1].astype(BF16))
    bias_i = bias_ref[:, 0:LANES]
    bias_f = bias_ref[:, LANES:2 * LANES]
    ones = jnp.ones((L, LANES), BF16)
    zeros = jnp.zeros((L, LANES), BF16)
    views = ((qf_ref, kf_ref, vf_ref, gf_ref, hf_ref), (qb_ref, kb_ref, vb_ref, gb_ref, hb_ref))

    for idx in range(nch):
        for d in range(2):
            q_ref, k_ref, v_ref, g_ref, h_ref = views[d]
            c0 = (idx if d == 0 else nch - 1 - idx) * L
            rows = slice(c0, c0 + L)
            li = g_ref[rows, 0:LANES] + bias_i
            cum = _cumsum_rows(tris[d], _log_sigmoid(g_ref[rows, LANES:2 * LANES] + bias_f))
            u = li - cum
            cm = _scan_rows(u, d == 1, jnp.maximum)
            u_t = u.T
            for hd in range(A_HEADS):
                chain = d * A_HEADS + hd
                b_bc = _lane_bcast(cum, chain)
                cm_bc = _lane_bcast(cm, chain)
                u_row = u_t[chain:chain + 1, :]
                bl = b_bc[L - 1:L, :] if d == 0 else b_bc[0:1, :]
                q = q_ref[rows, hd * A_QK:(hd + 1) * A_QK]
                k = k_ref[rows, hd * A_QK:(hd + 1) * A_QK]
                v = v_ref[rows, hd * A_V:(hd + 1) * A_V]
                v_aug = jnp.concatenate([v, ones], axis=1)
                v_zero = jnp.concatenate([v, zeros], axis=1)
                s = _dot_nt(q, k) * jnp.where(masks[d], jnp.exp(u_row - cm_bc), 0.0)
                g_row = u_row + bl
                g_max = jnp.max(g_row, axis=1, keepdims=True)
                kw_t = (k.T.astype(F32) * jnp.exp(g_row - g_max)).astype(BF16)
                dca = _dot(kw_t, v_aug)
                m_old = m_ref[chain]
                c_old = c_ref[chain]
                mx = jnp.maximum(m_old, cm_bc)
                a_loc = scale * jnp.exp(cm_bc - mx)
                a_int = scale * jnp.exp(m_old - mx)
                lhs = jnp.concatenate([(a_loc * s).astype(BF16), (a_int * q.astype(F32)).astype(BF16)], axis=1)
                nd = _dot(lhs, jnp.concatenate([v_zero, c_old.astype(BF16)], axis=0))
                den = a_loc * jnp.sum(s, axis=1, keepdims=True) + nd[:, A_V:]
                rden = 1.0 / jnp.maximum(jnp.abs(den), jnp.exp(-(b_bc + mx)))
                for c1 in range(0, A_V, LANES):
                    h_ref[rows, hd * A_V + c1:hd * A_V + c1 + LANES] = (nd[:, c1:c1 + LANES] * rden).astype(h_ref.dtype)
                g_max_b = jnp.broadcast_to(g_max, (1, LANES))
                m_new = jnp.maximum(bl + m_old, g_max_b)
                dec = jnp.exp(bl + m_old - m_new)
                inc = jnp.exp(g_max_b - m_new)
                n_tiles = A_VA // LANES
                c_ref[chain] = (jnp.concatenate([dec] * n_tiles, axis=1) * c_old
                                + jnp.concatenate([inc] * n_tiles, axis=1) * dca)
                m_ref[chain] = m_new


def _mlstm_mixer(p_main, gates, bias, batch, seq):
    m = batch * seq
    rows = min(MIX_ROWS, seq)
    nb = seq // rows
    fwd = lambda b, j: b * nb + j
    bwd = lambda b, j: b * nb + nb - 1 - j

    def specs(row):
        return [pl.BlockSpec((rows, 512), lambda b, j: (row(b, j), 0)),
                pl.BlockSpec((rows, 512), lambda b, j: (row(b, j), 1)),
                pl.BlockSpec((rows, 1024), lambda b, j: (row(b, j), 1)),
                pl.BlockSpec((rows, A_GATE_LANES), lambda b, j: (row(b, j), 0))]

    n_chain = 2 * A_HEADS
    return pl.pallas_call(
        functools.partial(_mlstm_kernel, chunk=A_CHUNK),
        grid=(batch, nb),
        in_specs=specs(fwd) + specs(bwd) + [_const_spec((1, A_GATE_LANES))],
        out_specs=[pl.BlockSpec((rows, D_MODEL), lambda b, j: (fwd(b, j), 0)),
                   pl.BlockSpec((rows, D_MODEL), lambda b, j: (bwd(b, j), 0))],
        out_shape=[jax.ShapeDtypeStruct((m, D_MODEL), BF16)] * 2,
        scratch_shapes=[pltpu.VMEM((n_chain, A_QK, A_VA), F32),
                        pltpu.VMEM((n_chain, 1, LANES), F32)],
        compiler_params=_params(("parallel", "arbitrary")),
        name="mlstm_mixer",
    )(p_main, p_main, p_main, gates, p_main, p_main, p_main, gates, bias)


def _hgrn_kernel(qf_ref, vf_ref, zf_ref, qb_ref, vb_ref, zb_ref, lb_ref, of_ref, ob_ref, st_ref, t_scr,
                 *, chunk, layer):
    L = chunk
    nch = qf_ref.shape[0] // L
    W = B_HEADS * B_F

    @pl.when(pl.program_id(1) == 0)
    def _():
        st_ref[...] = jnp.zeros_like(st_ref)

    n_slots = t_scr.shape[0]
    uses = [0]

    def dot_nt(a, b):
        slot = t_scr.at[uses[0] % n_slots]
        uses[0] += 1
        slot[...] = b.T
        return _dot(a, slot[...])

    row8 = lax.broadcasted_iota(jnp.int32, (SUBLANES, 1), 0)
    n_sub = L // SUBLANES
    k_sub = SUBLANES.bit_length() - 1
    small_ks = list(range(k_sub))
    big_ks = list(range(k_sub, L.bit_length() - 1))

    shr = lax.shift_right_logical
    row_i = lax.broadcasted_iota(jnp.int32, (SUBLANES, L), 0)
    lane_i = lax.broadcasted_iota(jnp.int32, (SUBLANES, L), 1)
    lane_in = lane_i & (SUBLANES - 1)
    eye_pat = row_i == lane_in
    lane_grp = {k: shr(lane_i, k) for k in big_ks}

    def small_pattern(k, d):
        same_pair = shr(row_i, k + 1) == shr(lane_in, k + 1)
        t_side = (shr(row_i, k) & 1) == (1 if d == 0 else 0)
        s_side = (shr(lane_in, k) & 1) == (0 if d == 0 else 1)
        return same_pair & t_side & s_side

    pats = [[small_pattern(k, d) for k in small_ks] for d in range(2)]

    lbs = []
    for d in range(2):
        p = lb_ref[d]
        e = jnp.exp(p - jnp.max(p, axis=0, keepdims=True))
        sm = e / jnp.sum(e, axis=0, keepdims=True)
        lb = jnp.zeros((1, W), F32)
        for r in range(1, layer + 1):
            lb = lb + sm[r:r + 1, :]
        lbs.append(lb)
    views = ((qf_ref, vf_ref, zf_ref, of_ref), (qb_ref, vb_ref, zb_ref, ob_ref))

    def boundary_rows(b, c, d):
        off = c - 1 if d == 0 else c
        if 2 * c >= SUBLANES:
            parts = [jnp.broadcast_to(b[p0 + off:p0 + off + 1, :], (2 * c, B_F)) for p0 in range(0, L, 2 * c)]
        else:
            parts = [jnp.where(row8 < 4,
                               jnp.broadcast_to(b[g + off:g + off + 1, :], (SUBLANES, B_F)),
                               jnp.broadcast_to(b[g + 4 + off:g + 5 + off, :], (SUBLANES, B_F)))
                     for g in range(0, L, SUBLANES)]
        return parts[0] if len(parts) == 1 else jnp.concatenate(parts, axis=0)

    for idx in range(nch):
        for d in range(2):
            q_ref, v_ref, z_ref, o_ref = views[d]
            c0 = (idx if d == 0 else nch - 1 - idx) * L
            rows = slice(c0, c0 + L)
            for h in range(B_HEADS):
                chain = d * B_HEADS + h
                fs = slice(h * B_F, (h + 1) * B_F)
                vs = slice(h * B_V, (h + 1) * B_V)
                lb = lbs[d][:, fs]
                z = z_ref[rows, fs]
                sg = _sigmoid(z)
                f = lb + (1.0 - lb) * sg
                kk = (1.0 - lb) * (1.0 - sg)
                b2 = _scan_rows(jnp.log2(f), d == 1, jnp.add)
                bl2 = b2[L - 1:L, :] if d == 0 else b2[0:1, :]
                q_bf = q_ref[rows, fs]
                q = q_bf.astype(F32)
                v_bf = v_ref[rows, vs]
                t_upper = d == 0

                def level_product(k):
                    c = 1 << k
                    if c >= SUBLANES:
                        lo_src, hi_src = (kk, q) if t_upper else (q, kk)
                        parts = []
                        for p0 in range(0, L, 2 * c):
                            parts += [lo_src[p0:p0 + c], hi_src[p0 + c:p0 + 2 * c]]
                        x = jnp.concatenate(parts, axis=0)
                    else:
                        is_upper = (shr(row8, k) & 1) == 1
                        is_t = is_upper if t_upper else jnp.logical_not(is_upper)
                        qq = q * f if c == 1 else q
                        x = jnp.concatenate([jnp.where(is_t, qq[g:g + SUBLANES], kk[g:g + SUBLANES])
                                             for g in range(0, L, SUBLANES)], axis=0)
                    if c > 1:
                        x = x * jnp.exp2(-jnp.abs(b2 - boundary_rows(b2, c, d)))
                    w = x.astype(BF16)
                    return dot_nt(w, w)

                def row_group(x, j):
                    return x[j * SUBLANES:(j + 1) * SUBLANES]

                prod = dot_nt(q_bf, kk.astype(BF16))
                acc = [jnp.where(eye_pat, row_group(prod, j), 0.0) for j in range(n_sub)]
                for k in small_ks:
                    prod = level_product(k)
                    acc = [jnp.where(pats[d][k], row_group(prod, j), acc[j]) for j in range(n_sub)]
                acc = [jnp.where(lane_grp[k_sub] == j, acc[j], 0.0) for j in range(n_sub)]
                for k in big_ks:
                    prod = level_product(k)
                    for j in range(n_sub):
                        grp = (j * SUBLANES) >> k
                        if (grp & 1) == (1 if t_upper else 0):
                            sib = grp - 1 if t_upper else grp + 1
                            acc[j] = jnp.where(lane_grp[k] == sib, row_group(prod, j), acc[j])
                a = jnp.concatenate(acc, axis=0)
                st = st_ref[chain]
                qe = (q * jnp.exp2(b2)).astype(BF16)
                ke = (kk * jnp.exp2(bl2 - b2)).astype(BF16)
                o_ref[rows, vs] = (dot_nt(qe, st.astype(BF16)) + _dot(a.astype(BF16), v_bf)).astype(o_ref.dtype)
                st_ref[chain] = st * jnp.exp2(bl2) + lax.dot_general(
                    v_bf, ke, (((0,), (0,)), ((), ())), preferred_element_type=F32)


def _hgrn_mixer(p_main, zgates, b_lb, layer, batch, seq):
    m = batch * seq
    rows = min(MIX_ROWS, seq)
    nb = seq // rows
    fwd = lambda b, j: b * nb + j
    bwd = lambda b, j: b * nb + nb - 1 - j

    def specs(row, zcol):
        return [pl.BlockSpec((rows, 1024), lambda b, j: (row(b, j), 0)),
                pl.BlockSpec((rows, 1024), lambda b, j: (row(b, j), 1)),
                pl.BlockSpec((rows, 1024), lambda b, j: (row(b, j), zcol))]

    return pl.pallas_call(
        functools.partial(_hgrn_kernel, chunk=B_CHUNK, layer=layer),
        grid=(batch, nb),
        in_specs=specs(fwd, 0) + specs(bwd, 1) + [_const_spec(b_lb.shape)],
        out_specs=[pl.BlockSpec((rows, D_MODEL), lambda b, j: (fwd(b, j), 0)),
                   pl.BlockSpec((rows, D_MODEL), lambda b, j: (bwd(b, j), 0))],
        out_shape=[jax.ShapeDtypeStruct((m, D_MODEL), BF16)] * 2,
        scratch_shapes=[pltpu.VMEM((2 * B_HEADS, B_V, B_F), F32),
                        pltpu.VMEM((T_SLOTS, B_F, B_CHUNK), BF16)],
        compiler_params=_params(("parallel", "arbitrary")),
        name="hgrn_mixer",
    )(p_main, p_main, zgates, p_main, p_main, zgates, b_lb)


def _mix_out_kernel(hf_ref, hb_ref, gate_ref, xq_ref, kv_ref, wo_ref, gh_ref, gp_ref, x_ref, o_ref,
                    *, head_dim, silu_gate):
    h = hf_ref[...].astype(F32) + hb_ref[...].astype(F32)
    parts = []
    for c0 in range(0, D_MODEL, head_dim):
        hh = h[:, c0:c0 + head_dim]
        parts.append(hh * lax.rsqrt(jnp.mean(hh * hh, axis=-1, keepdims=True) + EPS))
    gate = gate_ref[...].astype(F32)
    gate = gate * _sigmoid(gate) if silu_gate else _sigmoid(gate)
    mix = (jnp.concatenate(parts, axis=-1) * gh_ref[...] * gate).astype(BF16)
    y = _dot(mix, wo_ref[0:D_MODEL, :])

    xa = []
    for hd in range(X_HEADS):
        cs = slice(hd * X_HD, (hd + 1) * X_HD)
        s = _dot_nt(xq_ref[:, cs], kv_ref[:, cs]) * (X_HD ** -0.5)
        e = jnp.exp(s - jnp.max(s, axis=-1, keepdims=True))
        pv = _dot(e.astype(BF16), kv_ref[:, X_W + hd * X_HD:X_W + (hd + 1) * X_HD])
        xa.append((pv / jnp.sum(e, axis=-1, keepdims=True)).astype(BF16))
    y = y + _dot(jnp.concatenate(xa, axis=-1), wo_ref[D_MODEL:D_MODEL + X_W, :])
    o_ref[...] = x_ref[...] + _rms(y, gp_ref[...])


def _mix_out(hf, hb, p_main, kv, w_out, g_head, g_post, x2d, batch, seq, head_dim, silu_gate):
    m = batch * seq
    tm = min(ROW_TILE, seq)
    nb = seq // tm
    row = lambda b, j: b * nb + j
    tile = pl.BlockSpec((tm, D_MODEL), lambda b, j: (row(b, j), 0))
    return pl.pallas_call(
        functools.partial(_mix_out_kernel, head_dim=head_dim, silu_gate=silu_gate),
        grid=(batch, nb),
        in_specs=[tile, tile,
                  pl.BlockSpec((tm, D_MODEL), lambda b, j: (row(b, j), GATE_COL_BLOCK)),
                  pl.BlockSpec((tm, X_W), lambda b, j: (row(b, j), XQ_COL_BLOCK)),
                  pl.BlockSpec((N_MEM, 2 * X_W), lambda b, j: (b, 0)),
                  _const_spec(w_out.shape), _const_spec((1, D_MODEL)), _const_spec((1, D_MODEL)),
                  tile],
        out_specs=tile,
        out_shape=jax.ShapeDtypeStruct((m, D_MODEL), F32),
        compiler_params=_params(("parallel", "parallel")),
        name="mix_out",
    )(hf, hb, p_main, p_main, kv, w_out, g_head.reshape(1, D_MODEL), g_post.reshape(1, D_MODEL), x2d)


HALO = 16


def _ffn_kernel(x_ref, xp_ref, xn_ref, gpre_ref, wup_ref, cw_ref, cb_ref, wdn_ref, gpost_ref, o_ref,
                hext_ref, act_ref, *, d_ff, fc, groups):
    tm = x_ref.shape[0]
    j = pl.program_id(1)
    gpre = gpre_ref[...]
    x = x_ref[...]
    has_prev = (j > 0).astype(F32)
    has_next = (j < pl.num_programs(1) - 1).astype(F32)
    hext_ref[0:HALO, :] = (_rms(xp_ref[...], gpre) * has_prev).astype(BF16)
    hext_ref[HALO:HALO + tm, :] = _rms(x, gpre).astype(BF16)
    hext_ref[HALO + tm:HALO + tm + HALO, :] = (_rms(xn_ref[...], gpre) * has_next).astype(BF16)

    starts = list(range(0, d_ff, fc))
    per_group = -(-len(starts) // groups)
    n_ext = tm + 2 * HALO
    f = None
    for ci, c0 in enumerate(starts):
        cw = min(fc, d_ff - c0)
        conv = []
        for base in (c0, d_ff + c0):
            cs = slice(base, base + cw)
            ue = _dot(hext_ref[...], wup_ref[:, cs])
            u_prev = pltpu.roll(ue, 1, 0)[HALO:HALO + tm]
            u_next = pltpu.roll(ue, n_ext - 1, 0)[HALO:HALO + tm]
            conv.append(u_prev * cw_ref[0:1, cs] + ue[HALO:HALO + tm] * cw_ref[1:2, cs]
                        + u_next * cw_ref[2:3, cs] + cb_ref[:, cs])
        a, g = conv
        act_ref[:, c0:c0 + cw] = (g * _sigmoid(g) * a).astype(BF16)
        if (ci + 1) % per_group == 0 or ci == len(starts) - 1:
            k0 = starts[(ci // per_group) * per_group]
            part_f = _dot(act_ref[:, k0:c0 + cw], wdn_ref[k0:c0 + cw, :])
            f = part_f if f is None else f + part_f
    o_ref[...] = x + _rms(f, gpost_ref[...])


def _ffn(x2d, g_pre, w_up, conv_w, conv_b, w_down, g_post, batch, seq):
    m = batch * seq
    tm = min(FFN_TILE, seq)
    nb = seq // tm
    hb = tm // HALO
    last_halo = m // HALO - 1
    d_ff = w_down.shape[0]
    tile = pl.BlockSpec((tm, D_MODEL), lambda b, j: (b * nb + j, 0))
    return pl.pallas_call(
        functools.partial(_ffn_kernel, d_ff=d_ff, fc=FF_CHUNK, groups=FF_GROUPS),
        grid=(batch, nb),
        in_specs=[tile,
                  pl.BlockSpec((HALO, D_MODEL), lambda b, j: (jnp.maximum((b * nb + j) * hb - 1, 0), 0)),
                  pl.BlockSpec((HALO, D_MODEL), lambda b, j: (jnp.minimum((b * nb + j + 1) * hb, last_halo), 0)),
                  _const_spec((1, D_MODEL)), _const_spec(w_up.shape), _const_spec(conv_w.shape),
                  _const_spec((1, 2 * d_ff)), _const_spec(w_down.shape), _const_spec((1, D_MODEL))],
        out_specs=tile,
        out_shape=jax.ShapeDtypeStruct((m, D_MODEL), F32),
        scratch_shapes=[pltpu.VMEM((tm + 2 * HALO, D_MODEL), BF16),
                        pltpu.VMEM((tm, d_ff), BF16)],
        compiler_params=_params(("parallel", "parallel")),
        name="ffn",
    )(x2d, x2d, x2d, g_pre.reshape(1, D_MODEL), w_up, conv_w, conv_b.reshape(1, 2 * d_ff), w_down,
      g_post.reshape(1, D_MODEL))


def _encoder_stack(x, mem, g_pre_mix, g_post_mix, g_mem, a_w_main, a_w_gate, a_bias, a_g_head, a_w_out,
                   b_w_main, b_w_gate, b_lb, b_g_head, b_w_out, x_w_kv, g_pre_ffn, g_post_ffn,
                   f_w_up, f_conv_w, f_conv_b, f_w_down):
    batch, seq, d = x.shape
    x2d = x.reshape(batch * seq, d)
    mem2d = mem.reshape(batch * N_MEM, d)
    depth = g_pre_mix.shape[0]
    for i in range(depth):
        j = i // 2
        kv = _norm_proj(mem2d, g_mem[i], x_w_kv[i])
        if i % 2 == 0:
            p_main, gates = _norm_proj(x2d, g_pre_mix[i], a_w_main[j], a_w_gate[j])
            hf, hb = _mlstm_mixer(p_main, gates, a_bias[j], batch, seq)
            x2d = _mix_out(hf, hb, p_main, kv, a_w_out[j], a_g_head[j], g_post_mix[i], x2d, batch, seq,
                           head_dim=A_V, silu_gate=False)
        else:
            p_main, zgates = _norm_proj(x2d, g_pre_mix[i], b_w_main[j], b_w_gate[j])
            hf, hb = _hgrn_mixer(p_main, zgates, b_lb, j, batch, seq)
            x2d = _mix_out(hf, hb, p_main, kv, b_w_out[j], b_g_head[j], g_post_mix[i], x2d, batch, seq,
                           head_dim=B_V, silu_gate=True)
        x2d = _ffn(x2d, g_pre_ffn[i], f_w_up[i], f_conv_w[i], f_conv_b[i], f_w_down[i], g_post_ffn[i],
                   batch, seq)
    return x2d.reshape(batch, seq, d)


def kernel(x_prompt, x_sample, mem_prompt, mem_sample, g_pre_mix, g_post_mix, g_mem, a_w_in, a_b_gate, a_g_head, a_w_out, b_w_in, b_lb, b_g_head, b_w_out, x_w_kv, g_pre_ffn, g_post_ffn, f_w_up, f_conv_w, f_conv_b, f_w_down):
    a_rec = 2 * A_HEADS * A_QK + 2 * A_HEADS * A_V
    n_gate = 4 * A_HEADS
    a_w_main = jnp.concatenate([a_w_in[:, :, :a_rec], a_w_in[:, :, a_rec + n_gate:]], axis=-1).astype(BF16)
    def two_tiles(x):
        h = A_HEADS
        zeros = jnp.zeros(x.shape[:-1] + (LANES - 2 * h,), x.dtype)
        i_f, f_f, i_b, f_b = (x[..., n * h:(n + 1) * h] for n in range(4))
        return jnp.concatenate([i_f, i_b, zeros, f_f, f_b, zeros], axis=-1)

    a_w_gate = two_tiles(a_w_in[:, :, a_rec:a_rec + n_gate]).astype(BF16)
    a_bias = two_tiles(a_b_gate).reshape(-1, 1, A_GATE_LANES)
    w = B_HEADS * B_F
    b_w_main = jnp.concatenate([b_w_in[:, :, :2 * w], b_w_in[:, :, 4 * w:]], axis=-1).astype(BF16)
    b_w_gate = b_w_in[:, :, 2 * w:4 * w].astype(BF16)
    shared = (g_pre_mix, g_post_mix, g_mem, a_w_main, a_w_gate, a_bias, a_g_head, a_w_out.astype(BF16),
              b_w_main, b_w_gate, b_lb, b_g_head, b_w_out.astype(BF16), x_w_kv.astype(BF16),
              g_pre_ffn, g_post_ffn, f_w_up.astype(BF16), f_conv_w, f_conv_b, f_w_down.astype(BF16))
    return (_encoder_stack(x_prompt, mem_prompt, *shared), _encoder_stack(x_sample, mem_sample, *shared))
```
